```python
import math
import jax
import jax.numpy as jnp
from jax import lax
import numpy as np

D_MODEL = 2048
BATCH = 4
SEQ = 4096
DEPTH = 1

N_META = 16
MIX_WIDTH = D_MODEL
RW_WIDTH = MIX_WIDTH // 2
RW_HEAD = 64
RW_HEADS = RW_WIDTH // RW_HEAD
RW_LORA_W = 64
RW_LORA_A = 64
RW_GN_EPS = 64e-5
RW_SHIFT_COLS = 3 * RW_WIDTH + RW_LORA_W + RW_LORA_A
DN_WIDTH = MIX_WIDTH - RW_WIDTH
DN_HEAD = 128
DN_HEADS = DN_WIDTH // DN_HEAD
CONV_W = 4
CHUNK = 64
NORM_EPS = 1e-6
IN_COLS = RW_SHIFT_COLS + RW_WIDTH + 3 * DN_WIDTH + 2 * DN_HEADS + DN_WIDTH

kernel_name = "hymba_rwkv7_gated_deltanet_layer"


def rms_norm(x, w, eps=NORM_EPS):
    xf = x.astype(jnp.float32)
    y = xf * lax.rsqrt(jnp.mean(xf * xf, axis=-1, keepdims=True) + eps)
    return (y * w.astype(jnp.float32)).astype(x.dtype)


def l2_normalize(x, eps=1e-6):
    return x * lax.rsqrt(jnp.sum(x * x, axis=-1, keepdims=True) + eps)


def token_shift(p, mu):
    prev = jnp.pad(p, ((0, 0), (1, 0), (0, 0)))[:, :-1]
    return p + (prev - p) * mu


def rwkv7_mix(p_shift, gate, w0, w2, a0, a2, k_k, k_a, r_k, gn_w, gn_b):
    B, L, _ = p_shift.shape
    p_shift = p_shift.astype(jnp.float32)
    r, k, v, lw, la = jnp.split(
        p_shift, [RW_WIDTH, 2 * RW_WIDTH, 3 * RW_WIDTH, 3 * RW_WIDTH + RW_LORA_W], axis=-1)
    w_log = -jax.nn.softplus(-(w0 + jnp.tanh(lw) @ w2)) - 0.5
    decay = jnp.exp(-jnp.exp(w_log))
    a = jax.nn.sigmoid(a0 + la @ a2)
    hd = lambda t: t.reshape(B, L, RW_HEADS, RW_HEAD)
    kk = l2_normalize(hd(k * k_k))
    k = k * (1.0 + (a - 1.0) * k_a)
    rh, kh, vh, dh, ah = hd(r), hd(k), hd(v), hd(decay), hd(a)
    tm = lambda t: jnp.moveaxis(t, 1, 0)
    xs = (tm(rh), tm(dh), tm(kh), tm(vh), tm(kk), tm(kk * ah))

    def step(S, inp):
        r_t, w_t, k_t, v_t, kk_t, b_t = inp
        sa = jnp.einsum('bhvk,bhk->bhv', S, kk_t)
        S = (S * w_t[:, :, None, :] - sa[..., None] * b_t[:, :, None, :]
             + v_t[..., None] * k_t[:, :, None, :])
        y = jnp.einsum('bhvk,bhk->bhv', S, r_t)
        return S, y

    S0 = jnp.zeros((B, RW_HEADS, RW_HEAD, RW_HEAD), jnp.float32)
    _, y = lax.scan(step, S0, xs)
    y = jnp.moveaxis(y, 0, 1)
    mean = jnp.mean(y, axis=-1, keepdims=True)
    var = jnp.mean(jnp.square(y - mean), axis=-1, keepdims=True)
    y = ((y - mean) * lax.rsqrt(var + RW_GN_EPS)).reshape(B, L, RW_WIDTH) * gn_w + gn_b
    bonus = jnp.sum(rh * kh * hd(jnp.broadcast_to(r_k, r.shape)), axis=-1, keepdims=True) * vh
    y = y + bonus.reshape(B, L, RW_WIDTH)
    return y * jax.nn.silu(gate.astype(jnp.float32))


def causal_dwconv(u, w):
    C = u.shape[-1]
    return lax.conv_general_dilated(
        u, w[:, None, :], window_strides=(1,), padding=[(CONV_W - 1, 0)],
        dimension_numbers=('NWC', 'WIO', 'NWC'), feature_group_count=C)


def gated_delta_mix(qkv, b, alpha, z, conv_w, A_log, dt_bias, norm_w):
    B, L, _ = qkv.shape
    f32 = jnp.float32
    qkv = jax.nn.silu(causal_dwconv(qkv.astype(f32), conv_w.astype(f32)))
    q, k, v = jnp.split(qkv, [DN_WIDTH, 2 * DN_WIDTH], axis=-1)
    hd = lambda t: t.reshape(B, L, DN_HEADS, DN_HEAD)
    q = l2_normalize(hd(q)) * (DN_HEAD ** -0.5)
    k = l2_normalize(hd(k))
    v = hd(v)
    beta = jax.nn.sigmoid(b.astype(f32))
    g = -jnp.exp(A_log) * jax.nn.softplus(alpha.astype(f32) + dt_bias)

    pad_f = (-N_META) % CHUNK
    pad_b = (-(L + pad_f)) % CHUNK
    Lp = L + pad_f + pad_b
    Nc = Lp // CHUNK
    pad4 = ((0, 0), (pad_f, pad_b), (0, 0), (0, 0))
    pad3 = ((0, 0), (pad_f, pad_b), (0, 0))
    ch4 = lambda t: jnp.transpose(jnp.pad(t, pad4), (0, 2, 1, 3)).reshape(B, DN_HEADS, Nc, CHUNK, DN_HEAD)
    ch3 = lambda t: jnp.transpose(jnp.pad(t, pad3), (0, 2, 1)).reshape(B, DN_HEADS, Nc, CHUNK)
    q, k, v = ch4(q), ch4(k), ch4(v)
    beta, g = ch3(beta), ch3(g)

    g = jnp.cumsum(g, axis=-1)
    k_beta = k * beta[..., None]
    v_beta = v * beta[..., None]
    tri_incl = jnp.tril(jnp.ones((CHUNK, CHUNK), bool))
    tri_strict = jnp.tril(jnp.ones((CHUNK, CHUNK), bool), -1)
    decay_mask = jnp.exp(jnp.where(tri_incl, g[..., :, None] - g[..., None, :], -jnp.inf))
    M = jnp.where(tri_strict, jnp.einsum('bhncd,bhnsd->bhncs', k_beta, k) * decay_mask, 0.0)
    eye = jnp.eye(CHUNK, dtype=f32)
    T = lax.linalg.triangular_solve(M + eye, jnp.broadcast_to(eye, M.shape),
                                    left_side=True, lower=True, unit_diagonal=True)
    u = jnp.einsum('bhncs,bhnsd->bhncd', T, v_beta)
    w = jnp.einsum('bhncs,bhnsd->bhncd', T, k_beta * jnp.exp(g)[..., None])
    attn = jnp.where(tri_incl, jnp.einsum('bhncd,bhnsd->bhncs', q, k) * decay_mask, 0.0)

    def chunk_step(S, inp):
        q_c, k_c, u_c, w_c, g_c, a_c = inp
        v_new = u_c - jnp.einsum('bhck,bhkv->bhcv', w_c, S)
        o = (jnp.einsum('bhck,bhkv->bhcv', q_c * jnp.exp(g_c)[..., None], S)
             + jnp.einsum('bhcs,bhsv->bhcv', a_c, v_new))
        g_last = g_c[..., -1]
        S = (S * jnp.exp(g_last)[..., None, None]
             + jnp.einsum('bhck,bhcv->bhkv', k_c * jnp.exp(g_last[..., None] - g_c)[..., None], v_new))
        return S, o

    cm = lambda t: jnp.moveaxis(t, 2, 0)
    S0 = jnp.zeros((B, DN_HEADS, DN_HEAD, DN_HEAD), f32)
    _, o = lax.scan(chunk_step, S0, (cm(q), cm(k), cm(u), cm(w), cm(g), cm(attn)))
    o = jnp.moveaxis(o, 0, 2).reshape(B, DN_HEADS, Lp, DN_HEAD)
    o = jnp.transpose(o, (0, 2, 1, 3))[:, pad_f:pad_f + L]
    o = o * lax.rsqrt(jnp.mean(o * o, axis=-1, keepdims=True) + NORM_EPS) * norm_w
    o = o * jax.nn.silu(hd(z.astype(f32)))
    return o.reshape(B, L, DN_WIDTH)


def hybrid_layer(h, norm_w, w_in, shift_mu, rw_w0, rw_w2, rw_a0, rw_a2, rw_k_k, rw_k_a,
                 rw_r_k, rw_gn_w, rw_gn_b, dn_conv_w, dn_A_log, dn_dt_bias, dn_norm_w, w_out):
    u = rms_norm(h, norm_w)
    p = u @ w_in
    s1 = RW_SHIFT_COLS
    s2 = s1 + RW_WIDTH
    s3 = s2 + 3 * DN_WIDTH
    s4 = s3 + DN_HEADS
    s5 = s4 + DN_HEADS
    rw_p, rw_gate, dn_qkv, dn_b, dn_a, dn_z = jnp.split(p, [s1, s2, s3, s4, s5], axis=-1)
    y_a = rwkv7_mix(token_shift(rw_p.astype(jnp.float32), shift_mu), rw_gate, rw_w0, rw_w2,
                    rw_a0, rw_a2, rw_k_k, rw_k_a, rw_r_k, rw_gn_w, rw_gn_b)
    y_b = gated_delta_mix(dn_qkv, dn_b, dn_a, dn_z, dn_conv_w, dn_A_log, dn_dt_bias, dn_norm_w)
    y = jnp.concatenate([y_a, y_b], axis=-1).astype(h.dtype)
    return h + y @ w_out


def setup_inputs(seed: int = 0) -> dict:
    key = jax.random.key(seed)
    ks = jax.random.split(key, 20)
    f32 = jnp.float32
    nrm = lambda k, shape, s: s * jax.random.normal(k, shape, f32)
    x = nrm(ks[0], (BATCH, SEQ, D_MODEL), 1.0)
    meta_tokens = nrm(ks[1], (N_META, D_MODEL), 1.0)
    norm_w = 1.0 + nrm(ks[2], (DEPTH, D_MODEL), 0.02)
    w_in = nrm(ks[3], (DEPTH, D_MODEL, IN_COLS), D_MODEL ** -0.5)
    rw_shift_mu = jax.random.uniform(ks[4], (DEPTH, RW_SHIFT_COLS), f32)
    rw_w0 = jax.random.uniform(ks[5], (DEPTH, RW_WIDTH), f32, -6.0, 1.0)
    rw_w2 = nrm(ks[6], (DEPTH, RW_LORA_W, RW_WIDTH), 0.3 * RW_LORA_W ** -0.5)
    rw_a0 = nrm(ks[7], (DEPTH, RW_WIDTH), 0.1)
    rw_a2 = nrm(ks[8], (DEPTH, RW_LORA_A, RW_WIDTH), 0.3 * RW_LORA_A ** -0.5)
    rw_k_k = 0.85 + nrm(ks[9], (DEPTH, RW_WIDTH), 0.05)
    rw_k_a = 1.0 + nrm(ks[10], (DEPTH, RW_WIDTH), 0.05)
    rw_r_k = nrm(ks[11], (DEPTH, RW_WIDTH), 0.1)
    rw_gn_w = 1.0 + nrm(ks[12], (DEPTH, RW_WIDTH), 0.02)
    rw_gn_b = nrm(ks[13], (DEPTH, RW_WIDTH), 0.02)
    dn_conv_w = nrm(ks[14], (DEPTH, CONV_W, 3 * DN_WIDTH), CONV_W ** -0.5)
    dn_A_log = jnp.log(jax.random.uniform(ks[15], (DEPTH, DN_HEADS), f32, 1.0, 16.0))
    dt = jnp.exp(jax.random.uniform(ks[16], (DEPTH, DN_HEADS), f32, math.log(1e-3), math.log(1e-1)))
    dn_dt_bias = dt + jnp.log(-jnp.expm1(-dt))
    dn_norm_w = 1.0 + nrm(ks[17], (DEPTH, DN_HEAD), 0.02)
    w_out = nrm(ks[18], (DEPTH, MIX_WIDTH, D_MODEL), MIX_WIDTH ** -0.5)
    final_norm_w = 1.0 + nrm(ks[19], (D_MODEL,), 0.02)
    return {"x": x, "meta_tokens": meta_tokens, "norm_w": norm_w, "w_in": w_in,
            "rw_shift_mu": rw_shift_mu, "rw_w0": rw_w0, "rw_w2": rw_w2, "rw_a0": rw_a0,
            "rw_a2": rw_a2, "rw_k_k": rw_k_k, "rw_k_a": rw_k_a, "rw_r_k": rw_r_k,
            "rw_gn_w": rw_gn_w, "rw_gn_b": rw_gn_b, "dn_conv_w": dn_conv_w,
            "dn_A_log": dn_A_log, "dn_dt_bias": dn_dt_bias, "dn_norm_w": dn_norm_w,
            "w_out": w_out, "final_norm_w": final_norm_w}


def reference(x, meta_tokens, norm_w, w_in, rw_shift_mu, rw_w0, rw_w2, rw_a0, rw_a2,
              rw_k_k, rw_k_a, rw_r_k, rw_gn_w, rw_gn_b, dn_conv_w, dn_A_log, dn_dt_bias,
              dn_norm_w, w_out, final_norm_w):
    B = x.shape[0]
    meta = jnp.broadcast_to(meta_tokens.astype(x.dtype)[None], (B, N_META, x.shape[-1]))
    h = jnp.concatenate([meta, x], axis=1)
    for l in range(DEPTH):
        h = hybrid_layer(h, norm_w[l], w_in[l], rw_shift_mu[l], rw_w0[l], rw_w2[l], rw_a0[l],
                         rw_a2[l], rw_k_k[l], rw_k_a[l], rw_r_k[l], rw_gn_w[l], rw_gn_b[l],
                         dn_conv_w[l], dn_A_log[l], dn_dt_bias[l], dn_norm_w[l], w_out[l])
    h = rms_norm(h, final_norm_w)
    return h[:, N_META:]
```

```python
import functools

import jax
import jax.numpy as jnp
from jax import lax
from jax.experimental import pallas as pl
from jax.experimental.pallas import tpu as pltpu

F32 = jnp.float32
BF16 = jnp.bfloat16

LANE = 128
SUBLANE = 8
CH = 64
N_META = 16
RW_WIDTH = 1024
RW_HEAD = 64
RW_LORA = 64
DN_WIDTH = 1024
DN_HEAD = 128
DN_HEADS = DN_WIDTH // DN_HEAD
CONV_W = 4
NORM_EPS = 1e-6
L2_EPS = 1e-6
RW_GN_EPS = 64e-5
RW_G = RW_WIDTH // LANE
RW_PGROUPS = 4 * RW_G + 1
DN_PGROUPS = 4 * DN_HEADS + 1
P_GROUPS = RW_PGROUPS + DN_PGROUPS
VMEM_LIMIT = 56 * 1024 * 1024


def _mm(a, b):
    return jnp.dot(a.astype(BF16), b.astype(BF16), preferred_element_type=F32)


def _mm_nt(a, b):
    return lax.dot_general(a.astype(BF16), b.astype(BF16), (((1,), (1,)), ((), ())),
                           preferred_element_type=F32)


def _mm_tn(a, b):
    return lax.dot_general(a.astype(BF16), b.astype(BF16), (((0,), (0,)), ((), ())),
                           preferred_element_type=F32)


def _cumsum_rows(tri, x):
    h1 = x.astype(BF16)
    r1 = x - h1.astype(F32)
    h2 = r1.astype(BF16)
    h3 = (r1 - h2.astype(F32)).astype(BF16)
    dot = lambda h: jnp.dot(tri, h, preferred_element_type=F32)
    return dot(h1) + dot(h2) + dot(h3)


def _sigmoid(z):
    return 1.0 / (1.0 + jnp.exp(-z))


def _tri_inverse(a, n):
    i = lax.broadcasted_iota(jnp.int32, (n, n), 0)
    j = lax.broadcasted_iota(jnp.int32, (n, n), 1)
    eye = (i == j).astype(F32)

    def off(m):
        return ((i & -(2 * m)) == (j & -(2 * m))) & ((i & m) != 0) & ((j & m) == 0)

    t = eye - jnp.where(off(1), a, 0.0)
    m = 2
    while m < CH:
        a_off = jnp.where(off(m), a, 0.0)
        t = t - _mm(t, _mm(a_off, t))
        m *= 2
    return t


def _inproj_kernel(x_ref, nw_ref, w_ref, o_ref, u_ref, *, n_sub):
    @pl.when(pl.program_id(1) == 0)
    def _():
        x = x_ref[...]
        ms = jnp.mean(x * x, axis=-1, keepdims=True)
        u_ref[...] = (x * lax.rsqrt(ms + NORM_EPS) * nw_ref[...]).astype(BF16)

    acc = jnp.dot(u_ref[...], w_ref[...], preferred_element_type=F32)
    for s in range(n_sub):
        o_ref[s] = acc[:, s * LANE:(s + 1) * LANE]


def _inproj(x2d, norm_w, w_perm, tm, tn):
    m, d = x2d.shape
    n = w_perm.shape[1]
    n_sub = tn // LANE
    return pl.pallas_call(
        functools.partial(_inproj_kernel, n_sub=n_sub),
        grid=(m // tm, n // tn),
        in_specs=[
            pl.BlockSpec((tm, d), lambda i, j: (i, 0)),
            pl.BlockSpec((1, d), lambda i, j: (0, 0)),
            pl.BlockSpec((d, tn), lambda i, j: (0, j)),
        ],
        out_specs=pl.BlockSpec((n_sub, tm, LANE), lambda i, j: (j, i, 0)),
        out_shape=jax.ShapeDtypeStruct((n // LANE, m, LANE), F32),
        scratch_shapes=[pltpu.VMEM((tm, d), BF16)],
        compiler_params=pltpu.CompilerParams(
            dimension_semantics=("arbitrary", "arbitrary"), vmem_limit_bytes=VMEM_LIMIT),
        name="inproj",
    )(x2d, norm_w, w_perm)


def _rwkv_kernel(p_ref, mu_ref, w0_ref, a0_ref, kk_ref, ka_ref, rk_ref, gnw_ref, gnb_ref,
                 w2a2_ref, cin_ref, s0_ref, y_ref, cout_ref, sout_ref, s_ref, c_ref, lora_ref,
                 *, n_chunks):
    c = pl.program_id(1)

    @pl.when(c == 0)
    def _():
        s_ref[...] = s0_ref[...]
        c_ref[...] = cin_ref[...]

    row = lax.broadcasted_iota(jnp.int32, (CH, LANE), 0)
    lane = lax.broadcasted_iota(jnp.int32, (CH, LANE), 1)
    lo = lane < RW_HEAD
    i2 = lax.broadcasted_iota(jnp.int32, (2 * CH, 2 * CH), 0)
    j2 = lax.broadcasted_iota(jnp.int32, (2 * CH, 2 * CH), 1)
    same_head = (i2 >= CH) == (j2 >= CH)
    strict = same_head & (i2 > j2)
    incl = same_head & (i2 >= j2)
    ti = lax.broadcasted_iota(jnp.int32, (CH, CH), 0)
    tj = lax.broadcasted_iota(jnp.int32, (CH, CH), 1)
    tri = (ti >= tj).astype(BF16)

    def shift(x, tail):
        return jnp.where(row == 0, tail[SUBLANE - 1:SUBLANE, :], pltpu.roll(x, 1, 0))

    def seg_sum(x):
        s_lo = jnp.sum(jnp.where(lo, x, 0.0), axis=-1, keepdims=True)
        s_hi = jnp.sum(jnp.where(lo, 0.0, x), axis=-1, keepdims=True)
        return jnp.where(lo, s_lo, s_hi)

    def stack_masked(x):
        return jnp.concatenate([jnp.where(lo, x, 0.0), jnp.where(lo, 0.0, x)], axis=0)

    def stack(x):
        return jnp.concatenate([x, x], axis=0)

    def unstack(z):
        return jnp.where(lo, z[:CH], z[CH:])

    lg = p_ref[4 * RW_G]
    lg = lg + (shift(lg, c_ref[3 * RW_G]) - lg) * mu_ref[3 * RW_G]
    xl = jnp.where(lo, jnp.tanh(lg), lg)
    lora = jnp.dot(xl.astype(BF16), w2a2_ref[...], preferred_element_type=F32)
    for j in range(2 * RW_G):
        lora_ref[j] = lora[:, j * LANE:(j + 1) * LANE]

    def pair(g, carry):
        r = p_ref[g]
        k = p_ref[RW_G + g]
        v = p_ref[2 * RW_G + g]
        gate = p_ref[3 * RW_G + g]
        r = r + (shift(r, c_ref[g]) - r) * mu_ref[g]
        k = k + (shift(k, c_ref[RW_G + g]) - k) * mu_ref[RW_G + g]
        v = v + (shift(v, c_ref[2 * RW_G + g]) - v) * mu_ref[2 * RW_G + g]

        logw = -jnp.exp(-0.5) * _sigmoid(w0_ref[g] + lora_ref[g])
        a = _sigmoid(a0_ref[g] + lora_ref[RW_G + g])
        kx = k * kk_ref[g]
        kk = kx * lax.rsqrt(seg_sum(kx * kx) + L2_EPS)
        kmod = k * (1.0 + (a - 1.0) * ka_ref[g])
        bb = kk * a

        gam = _cumsum_rows(tri, logw)
        e_in = jnp.exp(gam)
        e_inv = jnp.exp(-gam)
        e_ex = jnp.exp(gam - logw)
        e_rem = jnp.exp(gam[CH - 1:CH, :] - gam)
        rt = r * e_in
        kt = kk * e_ex
        b_inv = bb * e_inv
        k_inv = kmod * e_inv

        lhs = jnp.concatenate([stack_masked(kt), stack_masked(rt)], axis=0)
        rhs = jnp.concatenate([stack(b_inv), stack(k_inv)], axis=0)
        a_all = _mm_nt(lhs, rhs)
        a_kb = jnp.where(strict, a_all[:2 * CH, :2 * CH], 0.0)
        a_kk = jnp.where(strict, a_all[:2 * CH, 2 * CH:], 0.0)
        a_rb = jnp.where(incl, a_all[2 * CH:, :2 * CH], 0.0)
        a_rk = jnp.where(incl, a_all[2 * CH:, 2 * CH:], 0.0)
        t_inv = _tri_inverse(a_kb, 2 * CH)

        av = _mm(jnp.concatenate([a_kk, a_rk], axis=0), stack(v))
        tx = _mm(t_inv, jnp.concatenate([stack(kt), av[:2 * CH]], axis=1))
        w_nat = unstack(tx[:, :LANE])
        u0 = unstack(tx[:, LANE:])
        y0 = unstack(av[2 * CH:])

        s_old = s_ref[g]
        wh = _mm_nt(jnp.concatenate([w_nat, rt], axis=0), s_old)
        u = wh[:CH] + u0
        y = wh[CH:] + y0 - unstack(_mm(a_rb, stack(u)))
        s_add = _mm_tn(jnp.concatenate([v, u], axis=0),
                       jnp.concatenate([kmod * e_rem, -(bb * e_rem)], axis=0))
        s_ref[g] = s_old * e_in[CH - 1:CH, :] + jnp.where(same_head, s_add, 0.0)

        mean = seg_sum(y) * (1.0 / RW_HEAD)
        d = y - mean
        var = seg_sum(d * d) * (1.0 / RW_HEAD)
        yn = d * lax.rsqrt(var + RW_GN_EPS) * gnw_ref[g] + gnb_ref[g]
        yn = yn + seg_sum(r * kmod * rk_ref[g]) * v
        y_ref[g] = (yn * gate * _sigmoid(gate)).astype(y_ref.dtype)
        return carry

    lax.fori_loop(0, RW_G, pair, 0)

    c_ref[0:3 * RW_G] = p_ref[0:3 * RW_G, CH - SUBLANE:CH, :]
    c_ref[3 * RW_G] = p_ref[4 * RW_G, CH - SUBLANE:CH, :]

    @pl.when(c == n_chunks - 1)
    def _():
        cout_ref[...] = c_ref[...]
        sout_ref[...] = s_ref[...]


def _rwkv(p, prm, carry_in, state_in, batch, n_chunks):
    m = batch * n_chunks * CH
    full = lambda shape: pl.BlockSpec(shape, lambda b, c: (0,) * len(shape))
    n_carry = 3 * RW_G + 1
    return pl.pallas_call(
        functools.partial(_rwkv_kernel, n_chunks=n_chunks),
        grid=(batch, n_chunks),
        in_specs=[
            pl.BlockSpec((RW_PGROUPS, CH, LANE), lambda b, c: (0, b * n_chunks + c, 0)),
            full((n_carry, 1, LANE)),
            full((RW_G, 1, LANE)), full((RW_G, 1, LANE)), full((RW_G, 1, LANE)),
            full((RW_G, 1, LANE)), full((RW_G, 1, LANE)), full((RW_G, 1, LANE)),
            full((RW_G, 1, LANE)),
            full((2 * RW_LORA, 2 * RW_WIDTH)),
            full((n_carry, SUBLANE, LANE)),
            full((RW_G, LANE, LANE)),
        ],
        out_specs=[
            pl.BlockSpec((RW_G, CH, LANE), lambda b, c: (0, b * n_chunks + c, 0)),
            full((n_carry, SUBLANE, LANE)),
            full((RW_G, LANE, LANE)),
        ],
        out_shape=[
            jax.ShapeDtypeStruct((RW_G, m, LANE), BF16),
            jax.ShapeDtypeStruct((n_carry, SUBLANE, LANE), F32),
            jax.ShapeDtypeStruct((RW_G, LANE, LANE), F32),
        ],
        scratch_shapes=[
            pltpu.VMEM((RW_G, LANE, LANE), F32),
            pltpu.VMEM((n_carry, SUBLANE, LANE), F32),
            pltpu.VMEM((2 * RW_G, CH, LANE), F32),
        ],
        compiler_params=pltpu.CompilerParams(
            dimension_semantics=("arbitrary", "arbitrary"), vmem_limit_bytes=VMEM_LIMIT),
        name="rwkv7_mix",
    )(p, prm["mu"], prm["w0"], prm["a0"], prm["k_k"], prm["k_a"], prm["r_k"], prm["gn_w"],
      prm["gn_b"], prm["w2a2"], carry_in, state_in)


def _gdn_kernel(p_ref, cw_ref, alog_ref, dtb_ref, nw_ref, cin_ref, s0_ref,
                y_ref, cout_ref, sout_ref, s_ref, c_ref, *, n_chunks):
    c = pl.program_id(1)

    @pl.when(c == 0)
    def _():
        s_ref[...] = s0_ref[...]
        c_ref[...] = cin_ref[...]

    row = lax.broadcasted_iota(jnp.int32, (CH, LANE), 0)
    lane2 = lax.broadcasted_iota(jnp.int32, (1, LANE), 1)
    lo = lane2 < CH
    i2 = lax.broadcasted_iota(jnp.int32, (2 * CH, 2 * CH), 0)
    j2 = lax.broadcasted_iota(jnp.int32, (2 * CH, 2 * CH), 1)
    same_head = (i2 >= CH) == (j2 >= CH)
    strict = same_head & (i2 > j2)
    incl = same_head & (i2 >= j2)
    ti = lax.broadcasted_iota(jnp.int32, (CH, CH), 0)
    tj = lax.broadcasted_iota(jnp.int32, (CH, CH), 1)
    tri = (ti >= tj).astype(BF16)

    def conv_silu(j):
        x = p_ref[j]
        tail = c_ref[j]
        acc = x * cw_ref[CONV_W - 1, j]
        for back in range(1, CONV_W):
            head_rows = jnp.concatenate([pltpu.roll(tail, back, 0)] * (CH // SUBLANE), axis=0)
            xs = jnp.where(row < back, head_rows, pltpu.roll(x, back, 0))
            acc = acc + xs * cw_ref[CONV_W - 1 - back, j]
        return acc * _sigmoid(acc)

    def l2n(x):
        return x * lax.rsqrt(jnp.sum(x * x, axis=-1, keepdims=True) + L2_EPS)

    ba = p_ref[4 * DN_HEADS]
    beta_all = _sigmoid(ba)
    z = ba + dtb_ref[...]
    softplus = jnp.maximum(z, 0.0) + jnp.log1p(jnp.exp(-jnp.abs(z)))
    g_all = -jnp.exp(alog_ref[...]) * softplus
    gc_all = _cumsum_rows(tri, g_all)
    gc_t = jnp.concatenate([gc_all, gc_all], axis=0).T

    for hp in range(DN_HEADS // 2):
        h1, h2 = 2 * hp, 2 * hp + 1
        qs, ks, vs, kbs, vbs, gcs = [], [], [], [], [], []
        for h in (h1, h2):
            q = l2n(conv_silu(h)) * (DN_HEAD ** -0.5)
            k = l2n(conv_silu(DN_HEADS + h))
            v = conv_silu(2 * DN_HEADS + h)
            beta = beta_all[:, h:h + 1]
            qs.append(q)
            ks.append(k)
            vs.append(v)
            kbs.append(k * beta)
            vbs.append(v * beta)
            gcs.append(gc_all[:, DN_HEADS + h:DN_HEADS + h + 1])
        zero = jnp.zeros((CH, LANE), F32)
        gc_col = jnp.concatenate(gcs, axis=0)
        gc_row = jnp.where(lo, gc_t[DN_HEADS + h1:DN_HEADS + h1 + 1, :],
                           gc_t[DN_HEADS + h2:DN_HEADS + h2 + 1, :])
        decay = jnp.where(incl, jnp.exp(gc_col - gc_row), 0.0)
        e_gc = [jnp.exp(gcs[0]), jnp.exp(gcs[1])]
        qe = [qs[0] * e_gc[0], qs[1] * e_gc[1]]

        lhs = jnp.concatenate([
            jnp.concatenate([kbs[0], zero], axis=1), jnp.concatenate([zero, kbs[1]], axis=1),
            jnp.concatenate([qs[0], zero], axis=1), jnp.concatenate([zero, qs[1]], axis=1)], axis=0)
        rhs = jnp.concatenate([
            jnp.concatenate([ks[0], zero], axis=1), jnp.concatenate([zero, ks[1]], axis=1)], axis=0)
        scores = _mm_nt(lhs, rhs)
        m_mat = jnp.where(strict, scores[:2 * CH] * decay, 0.0)
        attn = jnp.where(incl, scores[2 * CH:] * decay, 0.0)
        t_inv = _tri_inverse(m_mat, 2 * CH)

        x2 = jnp.concatenate([
            jnp.concatenate([vbs[0], kbs[0] * e_gc[0]], axis=1),
            jnp.concatenate([vbs[1], kbs[1] * e_gc[1]], axis=1)], axis=0)
        uw = _mm(t_inv, x2)

        v_new, o_state = [], []
        for e, h in enumerate((h1, h2)):
            s_old = s_ref[h]
            ws = _mm(jnp.concatenate([uw[e * CH:(e + 1) * CH, LANE:], qe[e]], axis=0), s_old)
            v_new.append(uw[e * CH:(e + 1) * CH, :LANE] - ws[:CH])
            o_state.append(ws[CH:])
        v_new2 = jnp.concatenate(v_new, axis=0)
        o2 = jnp.concatenate(o_state, axis=0) + _mm(attn, v_new2)

        g_last = [gcs[0][CH - 1:CH, :], gcs[1][CH - 1:CH, :]]
        k_dec = jnp.concatenate([
            jnp.concatenate([ks[0] * jnp.exp(g_last[0] - gcs[0]), zero], axis=1),
            jnp.concatenate([zero, ks[1] * jnp.exp(g_last[1] - gcs[1])], axis=1)], axis=0)
        s_add = _mm_tn(k_dec, v_new2)
        for e, h in enumerate((h1, h2)):
            s_ref[h] = s_ref[h] * jnp.exp(g_last[e]) + s_add[e * LANE:(e + 1) * LANE]
            o = o2[e * CH:(e + 1) * CH]
            o = o * lax.rsqrt(jnp.mean(o * o, axis=-1, keepdims=True) + NORM_EPS) * nw_ref[...]
            zg = p_ref[3 * DN_HEADS + h]
            y_ref[h] = (o * zg * _sigmoid(zg)).astype(y_ref.dtype)

    c_ref[...] = p_ref[0:3 * DN_HEADS, CH - SUBLANE:CH, :]

    @pl.when(c == n_chunks - 1)
    def _():
        cout_ref[...] = c_ref[...]
        sout_ref[...] = s_ref[...]


def _gdn(p, prm, carry_in, state_in, batch, n_chunks):
    m = batch * n_chunks * CH
    full = lambda shape: pl.BlockSpec(shape, lambda b, c: (0,) * len(shape))
    n_carry = 3 * DN_HEADS
    return pl.pallas_call(
        functools.partial(_gdn_kernel, n_chunks=n_chunks),
        grid=(batch, n_chunks),
        in_specs=[
            pl.BlockSpec((DN_PGROUPS, CH, LANE), lambda b, c: (1, b * n_chunks + c, 0)),
            full((CONV_W, n_carry, 1, LANE)),
            full((1, LANE)), full((1, LANE)), full((1, LANE)),
            full((n_carry, SUBLANE, LANE)),
            full((DN_HEADS, DN_HEAD, DN_HEAD)),
        ],
        out_specs=[
            pl.BlockSpec((DN_HEADS, CH, LANE), lambda b, c: (0, b * n_chunks + c, 0)),
            full((n_carry, SUBLANE, LANE)),
            full((DN_HEADS, DN_HEAD, DN_HEAD)),
        ],
        out_shape=[
            jax.ShapeDtypeStruct((DN_HEADS, m, LANE), BF16),
            jax.ShapeDtypeStruct((n_carry, SUBLANE, LANE), F32),
            jax.ShapeDtypeStruct((DN_HEADS, DN_HEAD, DN_HEAD), F32),
        ],
        scratch_shapes=[
            pltpu.VMEM((DN_HEADS, DN_HEAD, DN_HEAD), F32),
            pltpu.VMEM((n_carry, SUBLANE, LANE), F32),
        ],
        compiler_params=pltpu.CompilerParams(
            dimension_semantics=("arbitrary", "arbitrary"), vmem_limit_bytes=VMEM_LIMIT),
        name="gated_delta_mix",
    )(p, prm["conv_w"], prm["a_log"], prm["dt_bias"], prm["norm_w"], carry_in, state_in)


def _outproj_kernel(ya_ref, yb_ref, x_ref, w_ref, fnw_ref, o_ref):
    y = jnp.concatenate([ya_ref[g] for g in range(RW_G)] + [yb_ref[h] for h in range(DN_HEADS)],
                        axis=1)
    hid = x_ref[...] + jnp.dot(y, w_ref[...], preferred_element_type=F32)
    ms = jnp.mean(hid * hid, axis=-1, keepdims=True)
    o_ref[...] = hid * lax.rsqrt(ms + NORM_EPS) * fnw_ref[...]


def _outproj(ya, yb, x2d, w_out, fnw, tm):
    m, d = x2d.shape
    return pl.pallas_call(
        _outproj_kernel,
        grid=(m // tm,),
        in_specs=[
            pl.BlockSpec((RW_G, tm, LANE), lambda i: (0, i, 0)),
            pl.BlockSpec((DN_HEADS, tm, LANE), lambda i: (0, i, 0)),
            pl.BlockSpec((tm, d), lambda i: (i, 0)),
            pl.BlockSpec((RW_WIDTH + DN_WIDTH, d), lambda i: (0, 0)),
            pl.BlockSpec((1, d), lambda i: (0, 0)),
        ],
        out_specs=pl.BlockSpec((tm, d), lambda i: (i, 0)),
        out_shape=jax.ShapeDtypeStruct((m, d), F32),
        compiler_params=pltpu.CompilerParams(
            dimension_semantics=("arbitrary",), vmem_limit_bytes=VMEM_LIMIT),
        name="outproj",
    )(ya, yb, x2d, w_out, fnw)


def _row_tile(m, cap):
    t = cap
    while m % t:
        t //= 2
    return t


def kernel(x, meta_tokens, norm_w, w_in, rw_shift_mu, rw_w0, rw_w2, rw_a0, rw_a2, rw_k_k, rw_k_a,
           rw_r_k, rw_gn_w, rw_gn_b, dn_conv_w, dn_A_log, dn_dt_bias, dn_norm_w, w_out,
           final_norm_w):
    batch, seq, d = x.shape
    assert seq % CH == 0 and norm_w.shape[0] == 1
    n_chunks = seq // CH
    m = batch * seq

    wi = w_in[0]
    c_l = 3 * RW_WIDTH
    c_gate = c_l + 2 * RW_LORA
    c_qkv = c_gate + RW_WIDTH
    c_b = c_qkv + 3 * DN_WIDTH
    c_z = c_b + 2 * DN_HEADS
    w_perm = jnp.concatenate([
        wi[:, :c_l], wi[:, c_gate:c_qkv], wi[:, c_l:c_gate],
        wi[:, c_qkv:c_b], wi[:, c_z:], wi[:, c_b:c_z],
        jnp.zeros((d, LANE - 2 * DN_HEADS), wi.dtype)], axis=1).astype(BF16)
    grp = lambda t: t.reshape(-1, 1, LANE)
    zeros_l = jnp.zeros((RW_LORA, RW_WIDTH), F32)
    rw_prm = {
        "mu": grp(rw_shift_mu[0]),
        "w0": grp(rw_w0[0]), "a0": grp(rw_a0[0]), "k_k": grp(rw_k_k[0]), "k_a": grp(rw_k_a[0]),
        "r_k": grp(rw_r_k[0]), "gn_w": grp(rw_gn_w[0]), "gn_b": grp(rw_gn_b[0]),
        "w2a2": jnp.concatenate([jnp.concatenate([rw_w2[0], zeros_l], axis=1),
                                 jnp.concatenate([zeros_l, rw_a2[0]], axis=1)], axis=0).astype(BF16),
    }
    lane_vec = lambda t: jnp.zeros((1, LANE), F32).at[0, DN_HEADS:2 * DN_HEADS].set(t)
    dn_prm = {
        "conv_w": dn_conv_w[0].reshape(CONV_W, 3 * DN_HEADS, 1, LANE),
        "a_log": lane_vec(dn_A_log[0]), "dt_bias": lane_vec(dn_dt_bias[0]),
        "norm_w": dn_norm_w[0].reshape(1, LANE),
    }
    nw = norm_w[0].reshape(1, d)

    meta_rows = jnp.concatenate([jnp.zeros((CH - N_META, d), x.dtype), meta_tokens.astype(x.dtype)],
                                axis=0)
    tn = 768
    p_meta = _inproj(meta_rows, nw, w_perm, CH, tn)
    zc = lambda n: jnp.zeros((n, SUBLANE, LANE), F32)
    _, rw_carry, rw_state = _rwkv(p_meta, rw_prm, zc(3 * RW_G + 1),
                                  jnp.zeros((RW_G, LANE, LANE), F32), 1, 1)
    _, dn_carry, dn_state = _gdn(p_meta, dn_prm, zc(3 * DN_HEADS),
                                 jnp.zeros((DN_HEADS, DN_HEAD, DN_HEAD), F32), 1, 1)

    x2d = x.reshape(m, d)
    p = _inproj(x2d, nw, w_perm, _row_tile(m, 1024), tn)
    ya, _, _ = _rwkv(p, rw_prm, rw_carry, rw_state, batch, n_chunks)
    yb, _, _ = _gdn(p, dn_prm, dn_carry, dn_state, batch, n_chunks)
    out = _outproj(ya, yb, x2d, w_out[0].astype(BF16), final_norm_w.reshape(1, d),
                   _row_tile(m, 512))
    return out.reshape(batch, seq, d)
```

```python
import functools

import jax
import jax.numpy as jnp
from jax import lax
from jax.experimental import pallas as pl
from jax.experimental.pallas import tpu as pltpu

F32 = jnp.float32
BF16 = jnp.bfloat16

LANE = 128
SUBLANE = 8
CH = 64
N_META = 16
RW_WIDTH = 1024
RW_HEAD = 64
RW_LORA = 64
DN_WIDTH = 1024
DN_HEAD = 128
DN_HEADS = DN_WIDTH // DN_HEAD
CONV_W = 4
NORM_EPS = 1e-6
L2_EPS = 1e-6
RW_GN_EPS = 64e-5
RW_G = RW_WIDTH // LANE
RW_PGROUPS = 4 * RW_G + 1
DN_PGROUPS = 4 * DN_HEADS + 1
P_GROUPS = RW_PGROUPS + DN_PGROUPS
VMEM_LIMIT = 56 * 1024 * 1024


def _mm(a, b):
    return jnp.dot(a.astype(BF16), b.astype(BF16), preferred_element_type=F32)


def _mm_nt(a, b):
    return lax.dot_general(a.astype(BF16), b.astype(BF16), (((1,), (1,)), ((), ())),
                           preferred_element_type=F32)


def _mm_tn(a, b):
    return lax.dot_general(a.astype(BF16), b.astype(BF16), (((0,), (0,)), ((), ())),
                           preferred_element_type=F32)


def _each(fn, *lists):
    return [fn(*args) for args in zip(*lists)]


def _cumsum_rows(tri, x):
    h1 = x.astype(BF16)
    r1 = x - h1.astype(F32)
    h2 = r1.astype(BF16)
    h3 = (r1 - h2.astype(F32)).astype(BF16)
    dot = lambda h: jnp.dot(tri, h, preferred_element_type=F32)
    return dot(h1) + dot(h2) + dot(h3)


def _sigmoid(z):
    return 1.0 / (1.0 + jnp.exp(-z))


def _tri_inverse_each(mats):
    n = 2 * CH
    i = lax.broadcasted_iota(jnp.int32, (n, n), 0)
    j = lax.broadcasted_iota(jnp.int32, (n, n), 1)
    eye = (i == j).astype(F32)

    def off(m):
        return ((i & -(2 * m)) == (j & -(2 * m))) & ((i & m) != 0) & ((j & m) == 0)

    ts = [eye - jnp.where(off(1), a, 0.0) for a in mats]
    m = 2
    while m < CH:
        mask = off(m)
        prods = _each(lambda a, t: _mm(jnp.where(mask, a, 0.0), t), mats, ts)
        corr = _each(_mm, ts, prods)
        ts = _each(lambda t, c: t - c, ts, corr)
        m *= 2
    return ts


def _pair_masks():
    i2 = lax.broadcasted_iota(jnp.int32, (2 * CH, 2 * CH), 0)
    j2 = lax.broadcasted_iota(jnp.int32, (2 * CH, 2 * CH), 1)
    same_head = (i2 >= CH) == (j2 >= CH)
    return same_head, same_head & (i2 > j2), same_head & (i2 >= j2)


def _tri_ones():
    ti = lax.broadcasted_iota(jnp.int32, (CH, CH), 0)
    tj = lax.broadcasted_iota(jnp.int32, (CH, CH), 1)
    return (ti >= tj).astype(BF16)


def _inproj_kernel(x_ref, nw_ref, w_ref, o_ref, u_ref, *, n_sub):
    @pl.when(pl.program_id(1) == 0)
    def _():
        x = x_ref[...]
        ms = jnp.mean(x * x, axis=-1, keepdims=True)
        u_ref[...] = (x * lax.rsqrt(ms + NORM_EPS) * nw_ref[...]).astype(BF16)

    acc = jnp.dot(u_ref[...], w_ref[...], preferred_element_type=F32)
    for s in range(n_sub):
        o_ref[s] = acc[:, s * LANE:(s + 1) * LANE]


def _inproj(x2d, norm_w, w_perm, tm, tn):
    m, d = x2d.shape
    n = w_perm.shape[1]
    n_sub = tn // LANE
    return pl.pallas_call(
        functools.partial(_inproj_kernel, n_sub=n_sub),
        grid=(m // tm, n // tn),
        in_specs=[
            pl.BlockSpec((tm, d), lambda i, j: (i, 0)),
            pl.BlockSpec((1, d), lambda i, j: (0, 0)),
            pl.BlockSpec((d, tn), lambda i, j: (0, j)),
        ],
        out_specs=pl.BlockSpec((n_sub, tm, LANE), lambda i, j: (j, i, 0)),
        out_shape=jax.ShapeDtypeStruct((n // LANE, m, LANE), F32),
        scratch_shapes=[pltpu.VMEM((tm, d), BF16)],
        compiler_params=pltpu.CompilerParams(
            dimension_semantics=("arbitrary", "arbitrary"), vmem_limit_bytes=VMEM_LIMIT),
        name="inproj",
    )(x2d, norm_w, w_perm)


def _rwkv_kernel(p_ref, mu_ref, w0_ref, a0_ref, kk_ref, ka_ref, rk_ref, gnw_ref, gnb_ref,
                 w2a2_ref, cin_ref, s0_ref, y_ref, cout_ref, sout_ref, s_ref, c_ref,
                 *, n_chunks, rows):
    c = pl.program_id(1)

    @pl.when(c == 0)
    def _():
        for ri in range(rows):
            s_ref[ri] = s0_ref[...]
            c_ref[ri] = cin_ref[...]

    row = lax.broadcasted_iota(jnp.int32, (CH, LANE), 0)
    lane = lax.broadcasted_iota(jnp.int32, (CH, LANE), 1)
    lo = lane < RW_HEAD
    same_head, strict, incl = _pair_masks()
    tri = _tri_ones()
    items = [(ri, g) for ri in range(rows) for g in range(RW_G)]

    def shift(x, tail):
        return jnp.where(row == 0, tail[SUBLANE - 1:SUBLANE, :], pltpu.roll(x, 1, 0))

    def seg_sum(x):
        s_lo = jnp.sum(jnp.where(lo, x, 0.0), axis=-1, keepdims=True)
        s_hi = jnp.sum(jnp.where(lo, 0.0, x), axis=-1, keepdims=True)
        return jnp.where(lo, s_lo, s_hi)

    def stack_masked(x):
        return jnp.concatenate([jnp.where(lo, x, 0.0), jnp.where(lo, 0.0, x)], axis=0)

    def stack(x):
        return jnp.concatenate([x, x], axis=0)

    def unstack(z):
        return jnp.where(lo, z[:CH], z[CH:])

    def shifted(group, ri, mu_idx, tail_idx):
        x = p_ref[group, ri]
        return x + (shift(x, c_ref[ri, tail_idx]) - x) * mu_ref[mu_idx]

    loras = []
    for ri in range(rows):
        lg = shifted(4 * RW_G, ri, 3 * RW_G, 3 * RW_G)
        xl = jnp.where(lo, jnp.tanh(lg), lg)
        loras.append(jnp.dot(xl.astype(BF16), w2a2_ref[...], preferred_element_type=F32))

    r_s = [shifted(g, ri, g, g) for ri, g in items]
    k_s = [shifted(RW_G + g, ri, RW_G + g, RW_G + g) for ri, g in items]
    v_s = [shifted(2 * RW_G + g, ri, 2 * RW_G + g, 2 * RW_G + g) for ri, g in items]

    logw = [-jnp.exp(-0.5) * _sigmoid(w0_ref[g] + loras[ri][:, g * LANE:(g + 1) * LANE])
            for ri, g in items]
    a_s = [_sigmoid(a0_ref[g] + loras[ri][:, RW_WIDTH + g * LANE:RW_WIDTH + (g + 1) * LANE])
           for ri, g in items]

    def removal_key(k, item):
        kx = k * kk_ref[item[1]]
        return kx * lax.rsqrt(seg_sum(kx * kx) + L2_EPS)

    kk_s = _each(removal_key, k_s, items)
    kmod = _each(lambda k, a, item: k * (1.0 + (a - 1.0) * ka_ref[item[1]]), k_s, a_s, items)
    bb = _each(lambda kk, a: kk * a, kk_s, a_s)

    gam = []
    for ri in range(rows):
        wide = _cumsum_rows(tri, jnp.concatenate(logw[ri * RW_G:(ri + 1) * RW_G], axis=1))
        gam += [wide[:, g * LANE:(g + 1) * LANE] for g in range(RW_G)]

    e_in = _each(jnp.exp, gam)
    e_inv = _each(lambda x: jnp.exp(-x), gam)
    e_rem = _each(lambda x: jnp.exp(x[CH - 1:CH, :] - x), gam)
    rt = _each(lambda r, e: r * e, r_s, e_in)
    kt = _each(lambda kk, x, lw: kk * jnp.exp(x - lw), kk_s, gam, logw)
    b_inv = _each(lambda b, e: b * e, bb, e_inv)
    k_inv = _each(lambda k, e: k * e, kmod, e_inv)

    a_all = _each(
        lambda kt_, rt_, b_, k_: _mm_nt(
            jnp.concatenate([stack_masked(kt_), stack_masked(rt_)], axis=0),
            jnp.concatenate([stack(b_), stack(k_)], axis=0)),
        kt, rt, b_inv, k_inv)
    a_kb = [jnp.where(strict, a[:2 * CH, :2 * CH], 0.0) for a in a_all]
    a_kk = [jnp.where(strict, a[:2 * CH, 2 * CH:], 0.0) for a in a_all]
    a_rb = [jnp.where(incl, a[2 * CH:, :2 * CH], 0.0) for a in a_all]
    a_rk = [jnp.where(incl, a[2 * CH:, 2 * CH:], 0.0) for a in a_all]
    t_inv = _tri_inverse_each(a_kb)

    av = _each(lambda kk_, rk_, v: _mm(jnp.concatenate([kk_, rk_], axis=0), stack(v)),
               a_kk, a_rk, v_s)
    tx = _each(lambda t, kt_, av_: _mm(t, jnp.concatenate([stack(kt_), av_[:2 * CH]], axis=1)),
               t_inv, kt, av)
    w_nat = [unstack(x[:, :LANE]) for x in tx]
    u0 = [unstack(x[:, LANE:]) for x in tx]
    y0 = [unstack(x[2 * CH:]) for x in av]

    s_old = [s_ref[ri, g] for ri, g in items]
    wh = _each(lambda w, r, s: _mm_nt(jnp.concatenate([w, r], axis=0), s), w_nat, rt, s_old)
    u = _each(lambda wh_, u0_: wh_[:CH] + u0_, wh, u0)
    arbu = _each(lambda a, u_: unstack(_mm(a, stack(u_))), a_rb, u)
    y = _each(lambda wh_, y0_, x: wh_[CH:] + y0_ - x, wh, y0, arbu)
    s_add = _each(
        lambda v, u_, k, b, e: _mm_tn(jnp.concatenate([v, u_], axis=0),
                                      jnp.concatenate([k * e, -(b * e)], axis=0)),
        v_s, u, kmod, bb, e_rem)
    for (ri, g), s, e, add in zip(items, s_old, e_in, s_add):
        s_ref[ri, g] = s * e[CH - 1:CH, :] + jnp.where(same_head, add, 0.0)

    def finish(y_, r, k, v, item):
        ri, g = item
        mean = seg_sum(y_) * (1.0 / RW_HEAD)
        d = y_ - mean
        var = seg_sum(d * d) * (1.0 / RW_HEAD)
        yn = d * lax.rsqrt(var + RW_GN_EPS) * gnw_ref[g] + gnb_ref[g]
        yn = yn + seg_sum(r * k * rk_ref[g]) * v
        gate = p_ref[3 * RW_G + g, ri]
        y_ref[g, ri] = (yn * gate * _sigmoid(gate)).astype(y_ref.dtype)

    _each(finish, y, r_s, kmod, v_s, items)

    for ri in range(rows):
        c_ref[ri, 0:3 * RW_G] = p_ref[0:3 * RW_G, ri, CH - SUBLANE:CH, :]
        c_ref[ri, 3 * RW_G] = p_ref[4 * RW_G, ri, CH - SUBLANE:CH, :]

    @pl.when(c == n_chunks - 1)
    def _():
        cout_ref[...] = c_ref[rows - 1]
        sout_ref[...] = s_ref[rows - 1]


def _rwkv(p, prm, carry_in, state_in, rows):
    _, batch, seq, _ = p.shape
    n_chunks = seq // CH
    full = lambda shape: pl.BlockSpec(shape, lambda b, c: (0,) * len(shape))
    n_carry = 3 * RW_G + 1
    return pl.pallas_call(
        functools.partial(_rwkv_kernel, n_chunks=n_chunks, rows=rows),
        grid=(batch // rows, n_chunks),
        in_specs=[
            pl.BlockSpec((RW_PGROUPS, rows, CH, LANE), lambda b, c: (0, b, c, 0)),
            full((n_carry, 1, LANE)),
            full((RW_G, 1, LANE)), full((RW_G, 1, LANE)), full((RW_G, 1, LANE)),
            full((RW_G, 1, LANE)), full((RW_G, 1, LANE)), full((RW_G, 1, LANE)),
            full((RW_G, 1, LANE)),
            full((2 * RW_LORA, 2 * RW_WIDTH)),
            full((n_carry, SUBLANE, LANE)),
            full((RW_G, LANE, LANE)),
        ],
        out_specs=[
            pl.BlockSpec((RW_G, rows, CH, LANE), lambda b, c: (0, b, c, 0)),
            full((n_carry, SUBLANE, LANE)),
            full((RW_G, LANE, LANE)),
        ],
        out_shape=[
            jax.ShapeDtypeStruct((RW_G, batch, seq, LANE), BF16),
            jax.ShapeDtypeStruct((n_carry, SUBLANE, LANE), F32),
            jax.ShapeDtypeStruct((RW_G, LANE, LANE), F32),
        ],
        scratch_shapes=[
            pltpu.VMEM((rows, RW_G, LANE, LANE), F32),
            pltpu.VMEM((rows, n_carry, SUBLANE, LANE), F32),
        ],
        compiler_params=pltpu.CompilerParams(
            dimension_semantics=("arbitrary", "arbitrary"), vmem_limit_bytes=VMEM_LIMIT),
        name="rwkv7_mix",
    )(p, prm["mu"], prm["w0"], prm["a0"], prm["k_k"], prm["k_a"], prm["r_k"], prm["gn_w"],
      prm["gn_b"], prm["w2a2"], carry_in, state_in)


def _gdn_kernel(p_ref, cw_ref, alog_ref, dtb_ref, nw_ref, cin_ref, s0_ref,
                y_ref, cout_ref, sout_ref, s_ref, c_ref, *, n_chunks, rows):
    c = pl.program_id(1)

    @pl.when(c == 0)
    def _():
        for ri in range(rows):
            s_ref[ri] = s0_ref[...]
            c_ref[ri] = cin_ref[...]

    row = lax.broadcasted_iota(jnp.int32, (CH, LANE), 0)
    lo_row = lax.broadcasted_iota(jnp.int32, (1, LANE), 1) < CH
    same_head, strict, incl = _pair_masks()
    tri = _tri_ones()
    zero = jnp.zeros((CH, LANE), F32)
    heads = [(ri, h) for ri in range(rows) for h in range(DN_HEADS)]
    pairs = [(ri, hp) for ri in range(rows) for hp in range(DN_HEADS // 2)]
    first = lambda xs: xs[0::2]
    second = lambda xs: xs[1::2]

    def conv_silu(group, ri):
        x = p_ref[group, ri]
        tail = c_ref[ri, group]
        acc = x * cw_ref[CONV_W - 1, group]
        for back in range(1, CONV_W):
            head_rows = jnp.concatenate([pltpu.roll(tail, back, 0)] * (CH // SUBLANE), axis=0)
            xs = jnp.where(row < back, head_rows, pltpu.roll(x, back, 0))
            acc = acc + xs * cw_ref[CONV_W - 1 - back, group]
        return acc * _sigmoid(acc)

    def l2n(x):
        return x * lax.rsqrt(jnp.sum(x * x, axis=-1, keepdims=True) + L2_EPS)

    beta_all, gc_all, gc_t = [], [], []
    for ri in range(rows):
        ba = p_ref[4 * DN_HEADS, ri]
        beta_all.append(_sigmoid(ba))
        z = ba + dtb_ref[...]
        softplus = jnp.maximum(z, 0.0) + jnp.log1p(jnp.exp(-jnp.abs(z)))
        gc = _cumsum_rows(tri, -jnp.exp(alog_ref[...]) * softplus)
        gc_all.append(gc)
        gc_t.append(jnp.concatenate([gc, gc], axis=0).T)

    q = [l2n(conv_silu(h, ri)) * (DN_HEAD ** -0.5) for ri, h in heads]
    k = [l2n(conv_silu(DN_HEADS + h, ri)) for ri, h in heads]
    v = [conv_silu(2 * DN_HEADS + h, ri) for ri, h in heads]
    beta = [beta_all[ri][:, h:h + 1] for ri, h in heads]
    gc = [gc_all[ri][:, DN_HEADS + h:DN_HEADS + h + 1] for ri, h in heads]
    kb = _each(lambda x, b: x * b, k, beta)
    vb = _each(lambda x, b: x * b, v, beta)
    e_gc = _each(jnp.exp, gc)
    qe = _each(lambda x, e: x * e, q, e_gc)
    g_last = [x[CH - 1:CH, :] for x in gc]

    def decay_mask(item):
        ri, hp = item
        col = jnp.concatenate([gc[ri * DN_HEADS + 2 * hp], gc[ri * DN_HEADS + 2 * hp + 1]], axis=0)
        r1 = DN_HEADS + 2 * hp
        rowv = jnp.where(lo_row, gc_t[ri][r1:r1 + 1, :], gc_t[ri][r1 + 1:r1 + 2, :])
        return jnp.where(incl, jnp.exp(col - rowv), 0.0)

    decay = _each(decay_mask, pairs)

    def diag2(a, b):
        return jnp.concatenate([jnp.concatenate([a, zero], axis=1),
                                jnp.concatenate([zero, b], axis=1)], axis=0)

    scores = _each(
        lambda kb1, kb2, q1, q2, k1, k2: _mm_nt(
            jnp.concatenate([diag2(kb1, kb2), diag2(q1, q2)], axis=0), diag2(k1, k2)),
        first(kb), second(kb), first(q), second(q), first(k), second(k))
    m_mat = _each(lambda s, d: jnp.where(strict, s[:2 * CH] * d, 0.0), scores, decay)
    attn = _each(lambda s, d: jnp.where(incl, s[2 * CH:] * d, 0.0), scores, decay)
    t_inv = _tri_inverse_each(m_mat)

    kbe = _each(lambda x, e: x * e, kb, e_gc)
    uw = _each(
        lambda t, vb1, vb2, kbe1, kbe2: _mm(t, jnp.concatenate([
            jnp.concatenate([vb1, kbe1], axis=1), jnp.concatenate([vb2, kbe2], axis=1)], axis=0)),
        t_inv, first(vb), second(vb), first(kbe), second(kbe))

    s_old = [s_ref[ri, h] for ri, h in heads]
    u_h = [uw[i // 2][(i % 2) * CH:(i % 2 + 1) * CH, :LANE] for i in range(len(heads))]
    w_h = [uw[i // 2][(i % 2) * CH:(i % 2 + 1) * CH, LANE:] for i in range(len(heads))]
    ws = _each(lambda w, qe_, s: _mm(jnp.concatenate([w, qe_], axis=0), s), w_h, qe, s_old)
    v_new = _each(lambda u_, ws_: u_ - ws_[:CH], u_h, ws)
    v_new2 = _each(lambda a, b: jnp.concatenate([a, b], axis=0), first(v_new), second(v_new))
    o2 = _each(lambda ws1, ws2, a, vn: jnp.concatenate([ws1[CH:], ws2[CH:]], axis=0) + _mm(a, vn),
               first(ws), second(ws), attn, v_new2)

    k_dec = _each(lambda x, gl, g: x * jnp.exp(gl - g), k, g_last, gc)
    s_add = _each(lambda k1, k2, vn: _mm_tn(diag2(k1, k2), vn),
                  first(k_dec), second(k_dec), v_new2)
    for i, (ri, h) in enumerate(heads):
        s_ref[ri, h] = (s_old[i] * jnp.exp(g_last[i])
                        + s_add[i // 2][(i % 2) * LANE:(i % 2 + 1) * LANE])
        o = o2[i // 2][(i % 2) * CH:(i % 2 + 1) * CH]
        o = o * lax.rsqrt(jnp.mean(o * o, axis=-1, keepdims=True) + NORM_EPS) * nw_ref[...]
        zg = p_ref[3 * DN_HEADS + h, ri]
        y_ref[h, ri] = (o * zg * _sigmoid(zg)).astype(y_ref.dtype)

    for ri in range(rows):
        c_ref[ri] = p_ref[0:3 * DN_HEADS, ri, CH - SUBLANE:CH, :]

    @pl.when(c == n_chunks - 1)
    def _():
        cout_ref[...] = c_ref[rows - 1]
        sout_ref[...] = s_ref[rows - 1]


def _gdn(p, prm, carry_in, state_in, rows):
    _, batch, seq, _ = p.shape
    n_chunks = seq // CH
    full = lambda shape: pl.BlockSpec(shape, lambda b, c: (0,) * len(shape))
    n_carry = 3 * DN_HEADS
    return pl.pallas_call(
        functools.partial(_gdn_kernel, n_chunks=n_chunks, rows=rows),
        grid=(batch // rows, n_chunks),
        in_specs=[
            pl.BlockSpec((DN_PGROUPS, rows, CH, LANE), lambda b, c: (1, b, c, 0)),
            full((CONV_W, n_carry, 1, LANE)),
            full((1, LANE)), full((1, LANE)), full((1, LANE)),
            full((n_carry, SUBLANE, LANE)),
            full((DN_HEADS, DN_HEAD, DN_HEAD)),
        ],
        out_specs=[
            pl.BlockSpec((DN_HEADS, rows, CH, LANE), lambda b, c: (0, b, c, 0)),
            full((n_carry, SUBLANE, LANE)),
            full((DN_HEADS, DN_HEAD, DN_HEAD)),
        ],
        out_shape=[
            jax.ShapeDtypeStruct((DN_HEADS, batch, seq, LANE), BF16),
            jax.ShapeDtypeStruct((n_carry, SUBLANE, LANE), F32),
            jax.ShapeDtypeStruct((DN_HEADS, DN_HEAD, DN_HEAD), F32),
        ],
        scratch_shapes=[
            pltpu.VMEM((rows, DN_HEADS, DN_HEAD, DN_HEAD), F32),
            pltpu.VMEM((rows, n_carry, SUBLANE, LANE), F32),
        ],
        compiler_params=pltpu.CompilerParams(
            dimension_semantics=("arbitrary", "arbitrary"), vmem_limit_bytes=VMEM_LIMIT),
        name="gated_delta_mix",
    )(p, prm["conv_w"], prm["a_log"], prm["dt_bias"], prm["norm_w"], carry_in, state_in)


def _outproj_kernel(ya_ref, yb_ref, x_ref, w_ref, fnw_ref, o_ref):
    y = jnp.concatenate([ya_ref[g] for g in range(RW_G)] + [yb_ref[h] for h in range(DN_HEADS)],
                        axis=1)
    hid = x_ref[...] + jnp.dot(y, w_ref[...], preferred_element_type=F32)
    ms = jnp.mean(hid * hid, axis=-1, keepdims=True)
    o_ref[...] = hid * lax.rsqrt(ms + NORM_EPS) * fnw_ref[...]


def _outproj(ya, yb, x2d, w_out, fnw, tm):
    m, d = x2d.shape
    return pl.pallas_call(
        _outproj_kernel,
        grid=(m // tm,),
        in_specs=[
            pl.BlockSpec((RW_G, tm, LANE), lambda i: (0, i, 0)),
            pl.BlockSpec((DN_HEADS, tm, LANE), lambda i: (0, i, 0)),
            pl.BlockSpec((tm, d), lambda i: (i, 0)),
            pl.BlockSpec((RW_WIDTH + DN_WIDTH, d), lambda i: (0, 0)),
            pl.BlockSpec((1, d), lambda i: (0, 0)),
        ],
        out_specs=pl.BlockSpec((tm, d), lambda i: (i, 0)),
        out_shape=jax.ShapeDtypeStruct((m, d), F32),
        compiler_params=pltpu.CompilerParams(
            dimension_semantics=("arbitrary",), vmem_limit_bytes=VMEM_LIMIT),
        name="outproj",
    )(ya, yb, x2d, w_out, fnw)


def _row_tile(m, cap):
    t = cap
    while m % t:
        t //= 2
    return t


def kernel(x, meta_tokens, norm_w, w_in, rw_shift_mu, rw_w0, rw_w2, rw_a0, rw_a2, rw_k_k, rw_k_a,
           rw_r_k, rw_gn_w, rw_gn_b, dn_conv_w, dn_A_log, dn_dt_bias, dn_norm_w, w_out,
           final_norm_w):
    batch, seq, d = x.shape
    assert seq % CH == 0 and norm_w.shape[0] == 1
    m = batch * seq

    wi = w_in[0]
    c_l = 3 * RW_WIDTH
    c_gate = c_l + 2 * RW_LORA
    c_qkv = c_gate + RW_WIDTH
    c_b = c_qkv + 3 * DN_WIDTH
    c_z = c_b + 2 * DN_HEADS
    w_perm = jnp.concatenate([
        wi[:, :c_l], wi[:, c_gate:c_qkv], wi[:, c_l:c_gate],
        wi[:, c_qkv:c_b], wi[:, c_z:], wi[:, c_b:c_z],
        jnp.zeros((d, LANE - 2 * DN_HEADS), wi.dtype)], axis=1).astype(BF16)
    grp = lambda t: t.reshape(-1, 1, LANE)
    zeros_l = jnp.zeros((RW_LORA, RW_WIDTH), F32)
    rw_prm = {
        "mu": grp(rw_shift_mu[0]),
        "w0": grp(rw_w0[0]), "a0": grp(rw_a0[0]), "k_k": grp(rw_k_k[0]), "k_a": grp(rw_k_a[0]),
        "r_k": grp(rw_r_k[0]), "gn_w": grp(rw_gn_w[0]), "gn_b": grp(rw_gn_b[0]),
        "w2a2": jnp.concatenate([jnp.concatenate([rw_w2[0], zeros_l], axis=1),
                                 jnp.concatenate([zeros_l, rw_a2[0]], axis=1)], axis=0).astype(BF16),
    }
    lane_vec = lambda t: jnp.zeros((1, LANE), F32).at[0, DN_HEADS:2 * DN_HEADS].set(t)
    dn_prm = {
        "conv_w": dn_conv_w[0].reshape(CONV_W, 3 * DN_HEADS, 1, LANE),
        "a_log": lane_vec(dn_A_log[0]), "dt_bias": lane_vec(dn_dt_bias[0]),
        "norm_w": dn_norm_w[0].reshape(1, LANE),
    }
    nw = norm_w[0].reshape(1, d)

    meta_rows = jnp.concatenate([jnp.zeros((CH - N_META, d), x.dtype), meta_tokens.astype(x.dtype)],
                                axis=0)
    tn = 768
    p_meta = _inproj(meta_rows, nw, w_perm, CH, tn).reshape(P_GROUPS, 1, CH, LANE)
    zc = lambda n: jnp.zeros((n, SUBLANE, LANE), F32)
    _, rw_carry, rw_state = _rwkv(p_meta, rw_prm, zc(3 * RW_G + 1),
                                  jnp.zeros((RW_G, LANE, LANE), F32), 1)
    _, dn_carry, dn_state = _gdn(p_meta, dn_prm, zc(3 * DN_HEADS),
                                 jnp.zeros((DN_HEADS, DN_HEAD, DN_HEAD), F32), 1)

    x2d = x.reshape(m, d)
    p = _inproj(x2d, nw, w_perm, _row_tile(m, 1024), tn).reshape(P_GROUPS, batch, seq, LANE)
    ya, _, _ = _rwkv(p, rw_prm, rw_carry, rw_state, 1)
    yb, _, _ = _gdn(p, dn_prm, dn_carry, dn_state, 2 if batch % 2 == 0 else 1)
    out = _outproj(ya.reshape(RW_G, m, LANE), yb.reshape(DN_HEADS, m, LANE), x2d,
                   w_out[0].astype(BF16), final_norm_w.reshape(1, d), _row_tile(m, 512))
    return out.reshape(batch, seq, d)
```

```python
import functools

import jax
import jax.numpy as jnp
from jax import lax
from jax.experimental import pallas as pl
from jax.experimental.pallas import tpu as pltpu

F32 = jnp.float32
BF16 = jnp.bfloat16

LANE = 128
SUBLANE = 8
CH = 64
N_META = 16
RW_WIDTH = 1024
RW_HEAD = 64
RW_LORA = 64
DN_WIDTH = 1024
DN_HEAD = 128
DN_HEADS = DN_WIDTH // DN_HEAD
CONV_W = 4
NORM_EPS = 1e-6
L2_EPS = 1e-6
RW_GN_EPS = 64e-5
RW_G = RW_WIDTH // LANE
RW_PGROUPS = 4 * RW_G + 1
DN_PGROUPS = 4 * DN_HEADS + 1
P_GROUPS = RW_PGROUPS + DN_PGROUPS
VMEM_LIMIT = 56 * 1024 * 1024


def _mm(a, b):
    return jnp.dot(a.astype(BF16), b.astype(BF16), preferred_element_type=F32)


def _mm_nt(a, b):
    return lax.dot_general(a.astype(BF16), b.astype(BF16), (((1,), (1,)), ((), ())),
                           preferred_element_type=F32)


def _mm_tn(a, b):
    return lax.dot_general(a.astype(BF16), b.astype(BF16), (((0,), (0,)), ((), ())),
                           preferred_element_type=F32)


def _each(fn, *lists):
    return [fn(*args) for args in zip(*lists)]


def _cumsum_rows(tri, x):
    hi = x.astype(BF16)
    lo = (x - hi.astype(F32)).astype(BF16)
    dot = lambda h: jnp.dot(tri, h, preferred_element_type=F32)
    return dot(hi) + dot(lo)


def _sigmoid(z):
    return 0.5 * jnp.tanh(0.5 * z) + 0.5


def _shift_rows(x, tail, back):
    x3 = x.reshape(CH // SUBLANE, SUBLANE, LANE)
    rot = pltpu.roll(x3, back, 1)
    prev = jnp.concatenate([pltpu.roll(tail, back, 0)[None], rot[:-1]], axis=0)
    sub = lax.broadcasted_iota(jnp.int32, x3.shape, 1)
    return jnp.where(sub < back, prev, rot).reshape(CH, LANE)


def _tri_inverse_each(mats):
    n = 2 * CH
    i = lax.broadcasted_iota(jnp.int32, (n, n), 0)
    j = lax.broadcasted_iota(jnp.int32, (n, n), 1)
    eye = (i == j).astype(F32)

    def off(m):
        return ((i & -(2 * m)) == (j & -(2 * m))) & ((i & m) != 0) & ((j & m) == 0)

    ts = [eye - jnp.where(off(1), a, 0.0) for a in mats]
    m = 2
    while m < CH:
        mask = off(m)
        prods = _each(lambda a, t: _mm(jnp.where(mask, a, 0.0), t), mats, ts)
        corr = _each(_mm, ts, prods)
        ts = _each(lambda t, c: t - c, ts, corr)
        m *= 2
    return ts


def _pair_masks():
    i2 = lax.broadcasted_iota(jnp.int32, (2 * CH, 2 * CH), 0)
    j2 = lax.broadcasted_iota(jnp.int32, (2 * CH, 2 * CH), 1)
    same_head = (i2 >= CH) == (j2 >= CH)
    return same_head, same_head & (i2 > j2), same_head & (i2 >= j2)


def _tri_ones():
    ti = lax.broadcasted_iota(jnp.int32, (CH, CH), 0)
    tj = lax.broadcasted_iota(jnp.int32, (CH, CH), 1)
    return (ti >= tj).astype(BF16)


def _inproj_kernel(x_ref, nw_ref, w_ref, o_ref, u_ref, *, n_sub):
    @pl.when(pl.program_id(1) == 0)
    def _():
        x = x_ref[...]
        ms = jnp.mean(x * x, axis=-1, keepdims=True)
        u_ref[...] = (x * lax.rsqrt(ms + NORM_EPS) * nw_ref[...]).astype(BF16)

    acc = jnp.dot(u_ref[...], w_ref[...], preferred_element_type=F32)
    for s in range(n_sub):
        o_ref[s] = acc[:, s * LANE:(s + 1) * LANE]


def _inproj(x2d, norm_w, w_perm, tm, tn):
    m, d = x2d.shape
    n = w_perm.shape[1]
    n_sub = tn // LANE
    return pl.pallas_call(
        functools.partial(_inproj_kernel, n_sub=n_sub),
        grid=(m // tm, n // tn),
        in_specs=[
            pl.BlockSpec((tm, d), lambda i, j: (i, 0)),
            pl.BlockSpec((1, d), lambda i, j: (0, 0)),
            pl.BlockSpec((d, tn), lambda i, j: (0, j)),
        ],
        out_specs=pl.BlockSpec((n_sub, tm, LANE), lambda i, j: (j, i, 0)),
        out_shape=jax.ShapeDtypeStruct((n // LANE, m, LANE), F32),
        scratch_shapes=[pltpu.VMEM((tm, d), BF16)],
        compiler_params=pltpu.CompilerParams(
            dimension_semantics=("arbitrary", "arbitrary"), vmem_limit_bytes=VMEM_LIMIT),
        name="inproj",
    )(x2d, norm_w, w_perm)


def _rwkv_kernel(p_ref, mu_ref, w0_ref, a0_ref, kk_ref, ka_ref, rk_ref, gnw_ref, gnb_ref,
                 w2a2_ref, cin_ref, s0_ref, y_ref, cout_ref, sout_ref, s_ref, c_ref,
                 *, n_chunks, rows):
    c = pl.program_id(1)

    @pl.when(c == 0)
    def _():
        for ri in range(rows):
            s_ref[ri] = s0_ref[...]
            c_ref[ri] = cin_ref[...]

    lane = lax.broadcasted_iota(jnp.int32, (CH, LANE), 1)
    lo = lane < RW_HEAD
    same_head, strict, incl = _pair_masks()
    tri = _tri_ones()
    items = [(ri, g) for ri in range(rows) for g in range(RW_G)]

    def seg_sum(x):
        s_lo = jnp.sum(jnp.where(lo, x, 0.0), axis=-1, keepdims=True)
        s_hi = jnp.sum(jnp.where(lo, 0.0, x), axis=-1, keepdims=True)
        return jnp.where(lo, s_lo, s_hi)

    def stack_masked(x):
        return jnp.concatenate([jnp.where(lo, x, 0.0), jnp.where(lo, 0.0, x)], axis=0)

    def stack(x):
        return jnp.concatenate([x, x], axis=0)

    def unstack(z):
        return jnp.where(lo, z[:CH], z[CH:])

    def shifted(group, ri, mu_idx, tail_idx):
        x = p_ref[group, ri]
        return x + (_shift_rows(x, c_ref[ri, tail_idx], 1) - x) * mu_ref[mu_idx]

    loras = []
    for ri in range(rows):
        lg = shifted(4 * RW_G, ri, 3 * RW_G, 3 * RW_G)
        xl = jnp.where(lo, jnp.tanh(lg), lg)
        loras.append(jnp.dot(xl.astype(BF16), w2a2_ref[...], preferred_element_type=F32))

    r_s = [shifted(g, ri, g, g) for ri, g in items]
    k_s = [shifted(RW_G + g, ri, RW_G + g, RW_G + g) for ri, g in items]
    v_s = [shifted(2 * RW_G + g, ri, 2 * RW_G + g, 2 * RW_G + g) for ri, g in items]

    logw = [-jnp.exp(-0.5) * _sigmoid(w0_ref[g] + loras[ri][:, g * LANE:(g + 1) * LANE])
            for ri, g in items]
    a_s = [_sigmoid(a0_ref[g] + loras[ri][:, RW_WIDTH + g * LANE:RW_WIDTH + (g + 1) * LANE])
           for ri, g in items]

    def removal_key(k, item):
        kx = k * kk_ref[item[1]]
        return kx * lax.rsqrt(seg_sum(kx * kx) + L2_EPS)

    kk_s = _each(removal_key, k_s, items)
    kmod = _each(lambda k, a, item: k * (1.0 + (a - 1.0) * ka_ref[item[1]]), k_s, a_s, items)
    bb = _each(lambda kk, a: kk * a, kk_s, a_s)

    gam = []
    for ri in range(rows):
        wide = _cumsum_rows(tri, jnp.concatenate(logw[ri * RW_G:(ri + 1) * RW_G], axis=1))
        gam += [wide[:, g * LANE:(g + 1) * LANE] for g in range(RW_G)]

    e_in = _each(jnp.exp, gam)
    e_inv = _each(lambda x: jnp.exp(-x), gam)
    e_rem = _each(lambda x: jnp.exp(x[CH - 1:CH, :] - x), gam)
    rt = _each(lambda r, e: r * e, r_s, e_in)
    kt = _each(lambda kk, x, lw: kk * jnp.exp(x - lw), kk_s, gam, logw)
    b_inv = _each(lambda b, e: b * e, bb, e_inv)
    k_inv = _each(lambda k, e: k * e, kmod, e_inv)

    a_all = _each(
        lambda kt_, rt_, b_, k_: _mm_nt(
            jnp.concatenate([stack_masked(kt_), stack_masked(rt_)], axis=0),
            jnp.concatenate([stack(b_), stack(k_)], axis=0)),
        kt, rt, b_inv, k_inv)
    a_kb = [jnp.where(strict, a[:2 * CH, :2 * CH], 0.0) for a in a_all]
    a_kk = [jnp.where(strict, a[:2 * CH, 2 * CH:], 0.0) for a in a_all]
    a_rb = [jnp.where(incl, a[2 * CH:, :2 * CH], 0.0) for a in a_all]
    a_rk = [jnp.where(incl, a[2 * CH:, 2 * CH:], 0.0) for a in a_all]
    t_inv = _tri_inverse_each(a_kb)

    av = _each(lambda kk_, rk_, v: _mm(jnp.concatenate([kk_, rk_], axis=0), stack(v)),
               a_kk, a_rk, v_s)
    tx = _each(lambda t, kt_, av_: _mm(t, jnp.concatenate([stack(kt_), av_[:2 * CH]], axis=1)),
               t_inv, kt, av)
    w_nat = [unstack(x[:, :LANE]) for x in tx]
    u0 = [unstack(x[:, LANE:]) for x in tx]
    y0 = [unstack(x[2 * CH:]) for x in av]

    s_old = [s_ref[ri, g] for ri, g in items]
    wh = _each(lambda w, r, s: _mm_nt(jnp.concatenate([w, r], axis=0), s), w_nat, rt, s_old)
    u = _each(lambda wh_, u0_: wh_[:CH] + u0_, wh, u0)
    arbu = _each(lambda a, u_: unstack(_mm(a, stack(u_))), a_rb, u)
    y = _each(lambda wh_, y0_, x: wh_[CH:] + y0_ - x, wh, y0, arbu)
    s_add = _each(
        lambda v, u_, k, b, e: _mm_tn(jnp.concatenate([v, u_], axis=0),
                                      jnp.concatenate([k * e, -(b * e)], axis=0)),
        v_s, u, kmod, bb, e_rem)
    for (ri, g), s, e, add in zip(items, s_old, e_in, s_add):
        s_ref[ri, g] = s * e[CH - 1:CH, :] + jnp.where(same_head, add, 0.0)

    def finish(y_, r, k, v, item):
        ri, g = item
        mean = seg_sum(y_) * (1.0 / RW_HEAD)
        d = y_ - mean
        var = seg_sum(d * d) * (1.0 / RW_HEAD)
        yn = d * lax.rsqrt(var + RW_GN_EPS) * gnw_ref[g] + gnb_ref[g]
        yn = yn + seg_sum(r * k * rk_ref[g]) * v
        gate = p_ref[3 * RW_G + g, ri]
        y_ref[g, ri] = (yn * gate * _sigmoid(gate)).astype(y_ref.dtype)

    _each(finish, y, r_s, kmod, v_s, items)

    for ri in range(rows):
        c_ref[ri, 0:3 * RW_G] = p_ref[0:3 * RW_G, ri, CH - SUBLANE:CH, :]
        c_ref[ri, 3 * RW_G] = p_ref[4 * RW_G, ri, CH - SUBLANE:CH, :]

    @pl.when(c == n_chunks - 1)
    def _():
        cout_ref[...] = c_ref[rows - 1]
        sout_ref[...] = s_ref[rows - 1]


def _rwkv(p, prm, carry_in, state_in, rows):
    _, batch, seq, _ = p.shape
    n_chunks = seq // CH
    full = lambda shape: pl.BlockSpec(shape, lambda b, c: (0,) * len(shape))
    n_carry = 3 * RW_G + 1
    return pl.pallas_call(
        functools.partial(_rwkv_kernel, n_chunks=n_chunks, rows=rows),
        grid=(batch // rows, n_chunks),
        in_specs=[
            pl.BlockSpec((RW_PGROUPS, rows, CH, LANE), lambda b, c: (0, b, c, 0)),
            full((n_carry, 1, LANE)),
            full((RW_G, 1, LANE)), full((RW_G, 1, LANE)), full((RW_G, 1, LANE)),
            full((RW_G, 1, LANE)), full((RW_G, 1, LANE)), full((RW_G, 1, LANE)),
            full((RW_G, 1, LANE)),
            full((2 * RW_LORA, 2 * RW_WIDTH)),
            full((n_carry, SUBLANE, LANE)),
            full((RW_G, LANE, LANE)),
        ],
        out_specs=[
            pl.BlockSpec((RW_G, rows, CH, LANE), lambda b, c: (0, b, c, 0)),
            full((n_carry, SUBLANE, LANE)),
            full((RW_G, LANE, LANE)),
        ],
        out_shape=[
            jax.ShapeDtypeStruct((RW_G, batch, seq, LANE), BF16),
            jax.ShapeDtypeStruct((n_carry, SUBLANE, LANE), F32),
            jax.ShapeDtypeStruct((RW_G, LANE, LANE), F32),
        ],
        scratch_shapes=[
            pltpu.VMEM((rows, RW_G, LANE, LANE), F32),
            pltpu.VMEM((rows, n_carry, SUBLANE, LANE), F32),
        ],
        compiler_params=pltpu.CompilerParams(
            dimension_semantics=("arbitrary", "arbitrary"), vmem_limit_bytes=VMEM_LIMIT),
        name="rwkv7_mix",
    )(p, prm["mu"], prm["w0"], prm["a0"], prm["k_k"], prm["k_a"], prm["r_k"], prm["gn_w"],
      prm["gn_b"], prm["w2a2"], carry_in, state_in)


def _gdn_kernel(p_ref, cw_ref, alog_ref, dtb_ref, nw_ref, cin_ref, s0_ref,
                y_ref, cout_ref, sout_ref, s_ref, c_ref, *, n_chunks, rows):
    c = pl.program_id(1)

    @pl.when(c == 0)
    def _():
        for ri in range(rows):
            s_ref[ri] = s0_ref[...]
            c_ref[ri] = cin_ref[...]

    lo_row =lax.broadcasted_iota(jnp.int32, (1, LANE), 1) < CH
    same_head, strict, incl = _pair_masks()
    tri = _tri_ones()
    zero = jnp.zeros((CH, LANE), F32)
    heads = [(ri, h) for ri in range(rows) for h in range(DN_HEADS)]
    pairs = [(ri, hp) for ri in range(rows) for hp in range(DN_HEADS // 2)]
    first = lambda xs: xs[0::2]
    second = lambda xs: xs[1::2]

    def conv_silu(group, ri):
        x = p_ref[group, ri]
        tail = c_ref[ri, group]
        acc = x * cw_ref[CONV_W - 1, group]
        for back in range(1, CONV_W):
            acc = acc + _shift_rows(x, tail, back) * cw_ref[CONV_W - 1 - back, group]
        return acc * _sigmoid(acc)

    def l2n(x):
        return x * lax.rsqrt(jnp.sum(x * x, axis=-1, keepdims=True) + L2_EPS)

    beta_all, gc_all, gc_t = [], [], []
    for ri in range(rows):
        ba = p_ref[4 * DN_HEADS, ri]
        beta_all.append(_sigmoid(ba))
        z = ba + dtb_ref[...]
        softplus = jnp.maximum(z, 0.0) + jnp.log1p(jnp.exp(-jnp.abs(z)))
        gc = _cumsum_rows(tri, -jnp.exp(alog_ref[...]) * softplus)
        gc_all.append(gc)
        gc_t.append(jnp.concatenate([gc, gc], axis=0).T)

    q = [l2n(conv_silu(h, ri)) * (DN_HEAD ** -0.5) for ri, h in heads]
    k = [l2n(conv_silu(DN_HEADS + h, ri)) for ri, h in heads]
    v = [conv_silu(2 * DN_HEADS + h, ri) for ri, h in heads]
    beta = [beta_all[ri][:, h:h + 1] for ri, h in heads]
    gc = [gc_all[ri][:, DN_HEADS + h:DN_HEADS + h + 1] for ri, h in heads]
    kb = _each(lambda x, b: x * b, k, beta)
    vb = _each(lambda x, b: x * b, v, beta)
    e_gc = _each(jnp.exp, gc)
    qe = _each(lambda x, e: x * e, q, e_gc)
    g_last = [x[CH - 1:CH, :] for x in gc]

    def decay_mask(item):
        ri, hp = item
        col = jnp.concatenate([gc[ri * DN_HEADS + 2 * hp], gc[ri * DN_HEADS + 2 * hp + 1]], axis=0)
        r1 = DN_HEADS + 2 * hp
        rowv = jnp.where(lo_row, gc_t[ri][r1:r1 + 1, :], gc_t[ri][r1 + 1:r1 + 2, :])
        return jnp.where(incl, jnp.exp(col - rowv), 0.0)

    decay = _each(decay_mask, pairs)

    def diag2(a, b):
        return jnp.concatenate([jnp.concatenate([a, zero], axis=1),
                                jnp.concatenate([zero, b], axis=1)], axis=0)

    scores = _each(
        lambda kb1, kb2, q1, q2, k1, k2: _mm_nt(
            jnp.concatenate([diag2(kb1, kb2), diag2(q1, q2)], axis=0), diag2(k1, k2)),
        first(kb), second(kb), first(q), second(q), first(k), second(k))
    m_mat = _each(lambda s, d: jnp.where(strict, s[:2 * CH] * d, 0.0), scores, decay)
    attn = _each(lambda s, d: jnp.where(incl, s[2 * CH:] * d, 0.0), scores, decay)
    t_inv = _tri_inverse_each(m_mat)

    kbe = _each(lambda x, e: x * e, kb, e_gc)
    uw = _each(
        lambda t, vb1, vb2, kbe1, kbe2: _mm(t, jnp.concatenate([
            jnp.concatenate([vb1, kbe1], axis=1), jnp.concatenate([vb2, kbe2], axis=1)], axis=0)),
        t_inv, first(vb), second(vb), first(kbe), second(kbe))

    s_old = [s_ref[ri, h] for ri, h in heads]
    u_h = [uw[i // 2][(i % 2) * CH:(i % 2 + 1) * CH, :LANE] for i in range(len(heads))]
    w_h = [uw[i // 2][(i % 2) * CH:(i % 2 + 1) * CH, LANE:] for i in range(len(heads))]
    ws = _each(lambda w, qe_, s: _mm(jnp.concatenate([w, qe_], axis=0), s), w_h, qe, s_old)
    v_new = _each(lambda u_, ws_: u_ - ws_[:CH], u_h, ws)
    v_new2 = _each(lambda a, b: jnp.concatenate([a, b], axis=0), first(v_new), second(v_new))
    o2 = _each(lambda ws1, ws2, a, vn: jnp.concatenate([ws1[CH:], ws2[CH:]], axis=0) + _mm(a, vn),
               first(ws), second(ws), attn, v_new2)

    k_dec = _each(lambda x, gl, g: x * jnp.exp(gl - g), k, g_last, gc)
    s_add = _each(lambda k1, k2, vn: _mm_tn(diag2(k1, k2), vn),
                  first(k_dec), second(k_dec), v_new2)
    for i, (ri, h) in enumerate(heads):
        s_ref[ri, h] = (s_old[i] * jnp.exp(g_last[i])
                        + s_add[i // 2][(i % 2) * LANE:(i % 2 + 1) * LANE])
        o = o2[i // 2][(i % 2) * CH:(i % 2 + 1) * CH]
        o = o * lax.rsqrt(jnp.mean(o * o, axis=-1, keepdims=True) + NORM_EPS) * nw_ref[...]
        zg = p_ref[3 * DN_HEADS + h, ri]
        y_ref[h, ri] = (o * zg * _sigmoid(zg)).astype(y_ref.dtype)

    for ri in range(rows):
        c_ref[ri] = p_ref[0:3 * DN_HEADS, ri, CH - SUBLANE:CH, :]

    @pl.when(c == n_chunks - 1)
    def _():
        cout_ref[...] = c_ref[rows - 1]
        sout_ref[...] = s_ref[rows - 1]


def _gdn(p, prm, carry_in, state_in, rows):
    _, batch, seq, _ = p.shape
    n_chunks = seq // CH
    full = lambda shape: pl.BlockSpec(shape, lambda b, c: (0,) * len(shape))
    n_carry = 3 * DN_HEADS
    return pl.pallas_call(
        functools.partial(_gdn_kernel, n_chunks=n_chunks, rows=rows),
        grid=(batch // rows, n_chunks),
        in_specs=[
            pl.BlockSpec((DN_PGROUPS, rows, CH, LANE), lambda b, c: (1, b, c, 0)),
            full((CONV_W, n_carry, 1, LANE)),
            full((1, LANE)), full((1, LANE)), full((1, LANE)),
            full((n_carry, SUBLANE, LANE)),
            full((DN_HEADS, DN_HEAD, DN_HEAD)),
        ],
        out_specs=[
            pl.BlockSpec((DN_HEADS, rows, CH, LANE), lambda b, c: (0, b, c, 0)),
            full((n_carry, SUBLANE, LANE)),
            full((DN_HEADS, DN_HEAD, DN_HEAD)),
        ],
        out_shape=[
            jax.ShapeDtypeStruct((DN_HEADS, batch, seq, LANE), BF16),
            jax.ShapeDtypeStruct((n_carry, SUBLANE, LANE), F32),
            jax.ShapeDtypeStruct((DN_HEADS, DN_HEAD, DN_HEAD), F32),
        ],
        scratch_shapes=[
            pltpu.VMEM((rows, DN_HEADS, DN_HEAD, DN_HEAD), F32),
            pltpu.VMEM((rows, n_carry, SUBLANE, LANE), F32),
        ],
        compiler_params=pltpu.CompilerParams(
            dimension_semantics=("arbitrary", "arbitrary"), vmem_limit_bytes=VMEM_LIMIT),
        name="gated_delta_mix",
    )(p, prm["conv_w"], prm["a_log"], prm["dt_bias"], prm["norm_w"], carry_in, state_in)


def _outproj_kernel(ya_ref, yb_ref, x_ref, w_ref, fnw_ref, o_ref):
    y = jnp.concatenate([ya_ref[g] for g in range(RW_G)] + [yb_ref[h] for h in range(DN_HEADS)],
                        axis=1)
    hid = x_ref[...] + jnp.dot(y, w_ref[...], preferred_element_type=F32)
    ms = jnp.mean(hid * hid, axis=-1, keepdims=True)
    o_ref[...] = hid * lax.rsqrt(ms + NORM_EPS) * fnw_ref[...]


def _outproj(ya, yb, x2d, w_out, fnw, tm):
    m, d = x2d.shape
    return pl.pallas_call(
        _outproj_kernel,
        grid=(m // tm,),
        in_specs=[
            pl.BlockSpec((RW_G, tm, LANE), lambda i: (0, i, 0)),
            pl.BlockSpec((DN_HEADS, tm, LANE), lambda i: (0, i, 0)),
            pl.BlockSpec((tm, d), lambda i: (i, 0)),
            pl.BlockSpec((RW_WIDTH + DN_WIDTH, d), lambda i: (0, 0)),
            pl.BlockSpec((1, d), lambda i: (0, 0)),
        ],
        out_specs=pl.BlockSpec((tm, d), lambda i: (i, 0)),
        out_shape=jax.ShapeDtypeStruct((m, d), F32),
        compiler_params=pltpu.CompilerParams(
            dimension_semantics=("arbitrary",), vmem_limit_bytes=VMEM_LIMIT),
        name="outproj",
    )(ya, yb, x2d, w_out, fnw)


def _row_tile(m, cap):
    t = cap
    while m % t:
        t //= 2
    return t


def kernel(x, meta_tokens, norm_w, w_in, rw_shift_mu, rw_w0, rw_w2, rw_a0, rw_a2, rw_k_k, rw_k_a,
           rw_r_k, rw_gn_w, rw_gn_b, dn_conv_w, dn_A_log, dn_dt_bias, dn_norm_w, w_out,
           final_norm_w):
    batch, seq, d = x.shape
    assert seq % CH == 0 and norm_w.shape[0] == 1
    m = batch * seq

    wi = w_in[0].astype(BF16)
    c_l = 3 * RW_WIDTH
    c_gate = c_l + 2 * RW_LORA
    c_qkv = c_gate + RW_WIDTH
    c_b = c_qkv + 3 * DN_WIDTH
    c_z = c_b + 2 * DN_HEADS
    w_perm = jnp.concatenate([
        wi[:, :c_l], wi[:, c_gate:c_qkv], wi[:, c_l:c_gate],
        wi[:, c_qkv:c_b], wi[:, c_z:], wi[:, c_b:c_z],
        jnp.zeros((d, LANE - 2 * DN_HEADS), BF16)], axis=1)
    grp = lambda t: t.reshape(-1, 1, LANE)
    zeros_l = jnp.zeros((RW_LORA, RW_WIDTH), F32)
    rw_prm = {
        "mu": grp(rw_shift_mu[0]),
        "w0": grp(rw_w0[0]), "a0": grp(rw_a0[0]), "k_k": grp(rw_k_k[0]), "k_a": grp(rw_k_a[0]),
        "r_k": grp(rw_r_k[0]), "gn_w": grp(rw_gn_w[0]), "gn_b": grp(rw_gn_b[0]),
        "w2a2": jnp.concatenate([jnp.concatenate([rw_w2[0], zeros_l], axis=1),
                                 jnp.concatenate([zeros_l, rw_a2[0]], axis=1)], axis=0).astype(BF16),
    }
    lane_vec = lambda t: jnp.zeros((1, LANE), F32).at[0, DN_HEADS:2 * DN_HEADS].set(t)
    dn_prm = {
        "conv_w": dn_conv_w[0].reshape(CONV_W, 3 * DN_HEADS, 1, LANE),
        "a_log": lane_vec(dn_A_log[0]), "dt_bias": lane_vec(dn_dt_bias[0]),
        "norm_w": dn_norm_w[0].reshape(1, LANE),
    }
    nw = norm_w[0].reshape(1, d)

    meta_rows = jnp.concatenate([jnp.zeros((CH - N_META, d), x.dtype), meta_tokens.astype(x.dtype)],
                                axis=0)
    tn = 768
    p_meta = _inproj(meta_rows, nw, w_perm, CH, tn).reshape(P_GROUPS, 1, CH, LANE)
    zc = lambda n: jnp.zeros((n, SUBLANE, LANE), F32)
    _, rw_carry, rw_state = _rwkv(p_meta, rw_prm, zc(3 * RW_G + 1),
                                  jnp.zeros((RW_G, LANE, LANE), F32), 1)
    _, dn_carry, dn_state = _gdn(p_meta, dn_prm, zc(3 * DN_HEADS),
                                 jnp.zeros((DN_HEADS, DN_HEAD, DN_HEAD), F32), 1)

    x2d = x.reshape(m, d)
    p = _inproj(x2d, nw, w_perm, _row_tile(m, 1024), tn).reshape(P_GROUPS, batch, seq, LANE)
    ya, _, _ = _rwkv(p, rw_prm, rw_carry, rw_state, 2)
    yb, _, _ = _gdn(p, dn_prm, dn_carry, dn_state, 4 if batch % 4 == 0 else 1)
    out = _outproj(ya.reshape(RW_G, m, LANE), yb.reshape(DN_HEADS, m, LANE), x2d,
                   w_out[0].astype(BF16), final_norm_w.reshape(1, d), _row_tile(m, 512))
    return out.reshape(batch, seq, d)
```

```python
import functools

import jax
import jax.numpy as jnp
from jax import lax
from jax.experimental import pallas as pl
from jax.experimental.pallas import tpu as pltpu

F32 = jnp.float32
BF16 = jnp.bfloat16

LANE = 128
SUBLANE = 8
CH = 64
N_META = 16
RW_WIDTH = 1024
RW_HEAD = 64
RW_LORA = 64
DN_WIDTH = 1024
DN_HEAD = 128
DN_HEADS = DN_WIDTH // DN_HEAD
CONV_W = 4
NORM_EPS = 1e-6
L2_EPS = 1e-6
RW_GN_EPS = 64e-5
RW_G = RW_WIDTH // LANE
RW_PGROUPS = 4 * RW_G + 1
DN_PGROUPS = 4 * DN_HEADS + 1
P_GROUPS = RW_PGROUPS + DN_PGROUPS
VMEM_LIMIT = 56 * 1024 * 1024


def _mm(a, b):
    return jnp.dot(a.astype(BF16), b.astype(BF16), preferred_element_type=F32)


def _mm_nt(a, b):
    return lax.dot_general(a.astype(BF16), b.astype(BF16), (((1,), (1,)), ((), ())),
                           preferred_element_type=F32)


def _mm_tn(a, b):
    return lax.dot_general(a.astype(BF16), b.astype(BF16), (((0,), (0,)), ((), ())),
                           preferred_element_type=F32)


def _each(fn, *lists):
    return [fn(*args) for args in zip(*lists)]


def _cumsum_rows(tri, x):
    hi = x.astype(BF16)
    lo = (x - hi.astype(F32)).astype(BF16)
    dot = lambda h: jnp.dot(tri, h, preferred_element_type=F32)
    return dot(hi) + dot(lo)


def _sigmoid(z):
    return 0.5 * jnp.tanh(0.5 * z) + 0.5


def _shift_rows(x, tail, back):
    x3 = x.reshape(CH // SUBLANE, SUBLANE, LANE)
    rot = pltpu.roll(x3, back, 1)
    prev = jnp.concatenate([pltpu.roll(tail, back, 0)[None], rot[:-1]], axis=0)
    sub = lax.broadcasted_iota(jnp.int32, x3.shape, 1)
    return jnp.where(sub < back, prev, rot).reshape(CH, LANE)


def _pair_time_index():
    t = lax.broadcasted_iota(jnp.int32, (CH, 2 * CH), 0)
    lane = lax.broadcasted_iota(jnp.int32, (CH, 2 * CH), 1)
    return t, lane & (CH - 1), lane < CH


def _block_diag(x, first):
    xb = x.astype(BF16)
    zero = jnp.zeros_like(xb)
    return jnp.concatenate([jnp.where(first, xb, zero), jnp.where(first, zero, xb)], axis=0)


def _tri_inverse_each(mats):
    i, j, first = _pair_time_index()
    eye = (i == j).astype(BF16)
    zero = jnp.zeros_like(eye)

    def off(m):
        return ((i & -(2 * m)) == (j & -(2 * m))) & ((i & m) != 0) & ((j & m) == 0)

    neg = [(-a).astype(BF16) for a in mats]
    ts = [jnp.where(off(1), na, eye) for na in neg]
    m = 2
    while m < CH:
        mask = off(m)
        prods = _each(lambda na, t: _mm(jnp.where(mask, na, zero), _block_diag(t, first)), neg, ts)
        corr = _each(lambda t, p: _mm(t, _block_diag(p, first)), ts, prods)
        ts = _each(lambda t, c: jnp.where(mask, c.astype(BF16), t), ts, corr)
        m *= 2
    return ts


def _same_head_mask():
    i2 = lax.broadcasted_iota(jnp.int32, (2 * CH, 2 * CH), 0)
    j2 = lax.broadcasted_iota(jnp.int32, (2 * CH, 2 * CH), 1)
    return (i2 >= CH) == (j2 >= CH)


def _tri_ones():
    ti = lax.broadcasted_iota(jnp.int32, (CH, CH), 0)
    tj = lax.broadcasted_iota(jnp.int32, (CH, CH), 1)
    return (ti >= tj).astype(BF16)


def _inproj_kernel(x_ref, nw_ref, wa_ref, wb_ref, o_ref, u_ref, *, n_sub, n_main):
    j = pl.program_id(1)

    @pl.when(j == 0)
    def _():
        x = x_ref[...]
        ms = jnp.mean(x * x, axis=-1, keepdims=True)
        u_ref[...] = (x * lax.rsqrt(ms + NORM_EPS) * nw_ref[...]).astype(BF16)

    def project(w):
        acc = jnp.dot(u_ref[...], w, preferred_element_type=F32)
        for s in range(n_sub):
            o_ref[s] = acc[:, s * LANE:(s + 1) * LANE]

    @pl.when(j < n_main)
    def _():
        project(wa_ref[...].astype(BF16))

    @pl.when(j >= n_main)
    def _():
        project(wb_ref[...])


def _inproj(x2d, norm_w, w_main, w_tail, tm, tn):
    m, d = x2d.shape
    n_tail = w_tail.shape[1] // tn
    n_main = P_GROUPS * LANE // tn - n_tail
    n_sub = tn // LANE
    return pl.pallas_call(
        functools.partial(_inproj_kernel, n_sub=n_sub, n_main=n_main),
        grid=(m // tm, n_main + n_tail),
        in_specs=[
            pl.BlockSpec((tm, d), lambda i, j: (i, 0)),
            pl.BlockSpec((1, d), lambda i, j: (0, 0)),
            pl.BlockSpec((d, tn), lambda i, j: (0, jnp.minimum(j, n_main - 1))),
            pl.BlockSpec((d, tn), lambda i, j: (0, jnp.maximum(j - n_main, 0))),
        ],
        out_specs=pl.BlockSpec((n_sub, tm, LANE), lambda i, j: (j, i, 0)),
        out_shape=jax.ShapeDtypeStruct((P_GROUPS, m, LANE), F32),
        scratch_shapes=[pltpu.VMEM((tm, d), BF16)],
        compiler_params=pltpu.CompilerParams(
            dimension_semantics=("arbitrary", "arbitrary"), vmem_limit_bytes=VMEM_LIMIT),
        name="inproj",
    )(x2d, norm_w, w_main, w_tail)


def _rwkv_kernel(p_ref, mu_ref, w0_ref, a0_ref, kk_ref, ka_ref, rk_ref, gnw_ref, gnb_ref,
                 w2a2_ref, cin_ref, s0_ref, y_ref, cout_ref, sout_ref, s_ref, c_ref,
                 *, n_chunks, rows):
    c = pl.program_id(1)

    @pl.when(c == 0)
    def _():
        for ri in range(rows):
            s_ref[ri] = s0_ref[...]
            c_ref[ri] = cin_ref[...]

    _rwkv_rows(p_ref, mu_ref, w0_ref, a0_ref, kk_ref, ka_ref, rk_ref, gnw_ref, gnb_ref,
               w2a2_ref, y_ref, s_ref, c_ref, row_ids=tuple(range(rows)))

    for ri in range(rows):
        c_ref[ri] = p_ref[0:3 * RW_G + 1, ri, CH - SUBLANE:CH, :]

    @pl.when(c == n_chunks - 1)
    def _():
        cout_ref[...] = c_ref[rows - 1]
        sout_ref[...] = s_ref[rows - 1]


def _rwkv_rows(p_ref, mu_ref, w0_ref, a0_ref, kk_ref, ka_ref, rk_ref, gnw_ref, gnb_ref,
               w2a2_ref, y_ref, s_ref, c_ref, *, row_ids):
    t_i, s_i, lo = _pair_time_index()
    strict = t_i > s_i
    incl = t_i >= s_i
    same_head = _same_head_mask()
    tri = _tri_ones()
    items = [(ri, g) for ri in row_ids for g in range(RW_G)]
    bd = lambda x: _block_diag(x, lo)

    def seg_sum(x):
        s_lo = jnp.sum(jnp.where(lo, x, 0.0), axis=-1, keepdims=True)
        s_hi = jnp.sum(jnp.where(lo, 0.0, x), axis=-1, keepdims=True)
        return jnp.where(lo, s_lo, s_hi)

    def shifted(group, ri, mu_idx, tail_idx):
        x = p_ref[group, ri]
        return x + (_shift_rows(x, c_ref[ri, tail_idx], 1) - x) * mu_ref[mu_idx]

    loras = {}
    for ri in row_ids:
        lg = shifted(3 * RW_G, ri, 3 * RW_G, 3 * RW_G)
        xl = jnp.where(lo, jnp.tanh(lg), lg)
        loras[ri] = jnp.dot(xl.astype(BF16), w2a2_ref[...], preferred_element_type=F32)

    r_s = [shifted(g, ri, g, g) for ri, g in items]
    k_s = [shifted(RW_G + g, ri, RW_G + g, RW_G + g) for ri, g in items]
    v_s = [shifted(2 * RW_G + g, ri, 2 * RW_G + g, 2 * RW_G + g) for ri, g in items]

    logw = [-jnp.exp(-0.5) * _sigmoid(w0_ref[g] + loras[ri][:, g * LANE:(g + 1) * LANE])
            for ri, g in items]
    a_s = [_sigmoid(a0_ref[g] + loras[ri][:, RW_WIDTH + g * LANE:RW_WIDTH + (g + 1) * LANE])
           for ri, g in items]

    def removal_key(k, item):
        kx = k * kk_ref[item[1]]
        return kx * lax.rsqrt(seg_sum(kx * kx) + L2_EPS)

    kk_s = _each(removal_key, k_s, items)
    kmod = _each(lambda k, a, item: k * (1.0 + (a - 1.0) * ka_ref[item[1]]), k_s, a_s, items)
    bb = _each(lambda kk, a: kk * a, kk_s, a_s)

    gam = []
    for n in range(len(row_ids)):
        wide = _cumsum_rows(tri, jnp.concatenate(logw[n * RW_G:(n + 1) * RW_G], axis=1))
        gam += [wide[:, g * LANE:(g + 1) * LANE] for g in range(RW_G)]

    e_in = _each(jnp.exp, gam)
    e_inv = _each(lambda x: jnp.exp(-x), gam)
    e_rem = _each(lambda x: jnp.exp(x[CH - 1:CH, :] - x), gam)
    rt = _each(lambda r, e: r * e, r_s, e_in)
    kt = _each(lambda kk, x, lw: kk * jnp.exp(x - lw), kk_s, gam, logw)
    b_inv = _each(lambda b, e: b * e, bb, e_inv)
    k_inv = _each(lambda k, e: k * e, kmod, e_inv)

    a_all = _each(
        lambda kt_, rt_, b_, k_: _mm_nt(jnp.concatenate([kt_, rt_], axis=0),
                                        jnp.concatenate([bd(b_), bd(k_)], axis=0)),
        kt, rt, b_inv, k_inv)
    a_kb = [jnp.where(strict, a[:CH, :LANE], 0.0) for a in a_all]
    a_kk = [jnp.where(strict, a[:CH, LANE:], 0.0) for a in a_all]
    a_rb = [jnp.where(incl, a[CH:, :LANE], 0.0) for a in a_all]
    a_rk = [jnp.where(incl, a[CH:, LANE:], 0.0) for a in a_all]
    t_inv = _tri_inverse_each(a_kb)

    av = _each(lambda kk_, rk_, v: _mm(jnp.concatenate([kk_, rk_], axis=0), bd(v)),
               a_kk, a_rk, v_s)
    tx = _each(lambda t, kt_, av_: _mm(t, jnp.concatenate([bd(kt_), bd(av_[:CH])], axis=1)),
               t_inv, kt, av)
    w_nat = [x[:, :LANE] for x in tx]
    u0 = [x[:, LANE:] for x in tx]
    y0 = [x[CH:] for x in av]

    s_old = [s_ref[ri, g] for ri, g in items]
    wh = _each(lambda w, r, s: _mm_nt(jnp.concatenate([w, r], axis=0), s), w_nat, rt, s_old)
    u = _each(lambda wh_, u0_: wh_[:CH] + u0_, wh, u0)
    arbu = _each(lambda a, u_: _mm(a, bd(u_)), a_rb, u)
    y = _each(lambda wh_, y0_, x: wh_[CH:] + y0_ - x, wh, y0, arbu)
    s_add = _each(
        lambda v, u_, k, b, e: _mm_tn(jnp.concatenate([v, u_], axis=0),
                                      jnp.concatenate([k * e, -(b * e)], axis=0)),
        v_s, u, kmod, bb, e_rem)
    for (ri, g), s, e, add in zip(items, s_old, e_in, s_add):
        s_ref[ri, g] = s * e[CH - 1:CH, :] + jnp.where(same_head, add, 0.0)

    def finish(y_, r, k, v, item):
        ri, g = item
        mean = seg_sum(y_) * (1.0 / RW_HEAD)
        d = y_ - mean
        var = seg_sum(d * d) * (1.0 / RW_HEAD)
        yn = d * lax.rsqrt(var + RW_GN_EPS) * gnw_ref[g] + gnb_ref[g]
        yn = yn + seg_sum(r * k * rk_ref[g]) * v
        gate = p_ref[3 * RW_G + 1 + g, ri]
        y_ref[g, ri] = (yn * gate * _sigmoid(gate)).astype(y_ref.dtype)

    _each(finish, y, r_s, kmod, v_s, items)


def _rwkv(p, prm, carry_in, state_in, rows):
    _, batch, seq, _ = p.shape
    n_chunks = seq // CH
    full = lambda shape: pl.BlockSpec(shape, lambda b, c: (0,) * len(shape))
    n_carry = 3 * RW_G + 1
    return pl.pallas_call(
        functools.partial(_rwkv_kernel, n_chunks=n_chunks, rows=rows),
        grid=(batch // rows, n_chunks),
        in_specs=[
            pl.BlockSpec((RW_PGROUPS, rows, CH, LANE), lambda b, c: (0, b, c, 0)),
            full((n_carry, 1, LANE)),
            full((RW_G, 1, LANE)), full((RW_G, 1, LANE)), full((RW_G, 1, LANE)),
            full((RW_G, 1, LANE)), full((RW_G, 1, LANE)), full((RW_G, 1, LANE)),
            full((RW_G, 1, LANE)),
            full((2 * RW_LORA, 2 * RW_WIDTH)),
            full((n_carry, SUBLANE, LANE)),
            full((RW_G, LANE, LANE)),
        ],
        out_specs=[
            pl.BlockSpec((RW_G, rows, CH, LANE), lambda b, c: (0, b, c, 0)),
            full((n_carry, SUBLANE, LANE)),
            full((RW_G, LANE, LANE)),
        ],
        out_shape=[
            jax.ShapeDtypeStruct((RW_G, batch, seq, LANE), BF16),
            jax.ShapeDtypeStruct((n_carry, SUBLANE, LANE), F32),
            jax.ShapeDtypeStruct((RW_G, LANE, LANE), F32),
        ],
        scratch_shapes=[
            pltpu.VMEM((rows, RW_G, LANE, LANE), F32),
            pltpu.VMEM((rows, n_carry, SUBLANE, LANE), F32),
        ],
        compiler_params=pltpu.CompilerParams(
            dimension_semantics=("arbitrary", "arbitrary"), vmem_limit_bytes=VMEM_LIMIT),
        name="rwkv7_mix",
    )(p, prm["mu"], prm["w0"], prm["a0"], prm["k_k"], prm["k_a"], prm["r_k"], prm["gn_w"],
      prm["gn_b"], prm["w2a2"], carry_in, state_in)


def _gdn_kernel(p_ref, cw_ref, alog_ref, dtb_ref, nw_ref, cin_ref, s0_ref,
                y_ref, cout_ref, sout_ref, s_ref, c_ref, *, n_chunks, rows):
    c = pl.program_id(1)

    @pl.when(c == 0)
    def _():
        for ri in range(rows):
            s_ref[ri] = s0_ref[...]
            c_ref[ri] = cin_ref[...]

    _gdn_rows(p_ref, cw_ref, alog_ref, dtb_ref, nw_ref, y_ref, s_ref, c_ref,
              row_ids=tuple(range(rows)))

    for ri in range(rows):
        c_ref[ri] = p_ref[0:3 * DN_HEADS, ri, CH - SUBLANE:CH, :]

    @pl.when(c == n_chunks - 1)
    def _():
        cout_ref[...] = c_ref[rows - 1]
        sout_ref[...] = s_ref[rows - 1]


def _gdn_rows(p_ref, cw_ref, alog_ref, dtb_ref, nw_ref, y_ref, s_ref, c_ref, *, row_ids):
    t_i, s_i, lo = _pair_time_index()
    strict = t_i > s_i
    incl = t_i >= s_i
    lo_row = lo[:1]
    tri = _tri_ones()
    zero = jnp.zeros((CH, LANE), F32)
    heads = [(ri, h) for ri in row_ids for h in range(DN_HEADS)]
    pairs = [(n, ri, hp) for n, ri in enumerate(row_ids) for hp in range(DN_HEADS // 2)]
    first = lambda xs: xs[0::2]
    second = lambda xs: xs[1::2]

    def conv_silu(group, ri):
        x = p_ref[group, ri]
        tail = c_ref[ri, group]
        acc = x * cw_ref[CONV_W - 1, group]
        for back in range(1, CONV_W):
            acc = acc + _shift_rows(x, tail, back) * cw_ref[CONV_W - 1 - back, group]
        return acc * _sigmoid(acc)

    def l2n(x):
        return x * lax.rsqrt(jnp.sum(x * x, axis=-1, keepdims=True) + L2_EPS)

    beta_all, gc_all, gc_t = {}, {}, {}
    for ri in row_ids:
        ba = p_ref[4 * DN_HEADS, ri]
        beta_all[ri] = _sigmoid(ba)
        z = ba + dtb_ref[...]
        softplus = jnp.maximum(z, 0.0) + jnp.log1p(jnp.exp(-jnp.abs(z)))
        gc_all[ri] = _cumsum_rows(tri, -jnp.exp(alog_ref[...]) * softplus)
        gc_t[ri] = jnp.concatenate([gc_all[ri], gc_all[ri]], axis=0).T

    q = [l2n(conv_silu(h, ri)) * (DN_HEAD ** -0.5) for ri, h in heads]
    k = [l2n(conv_silu(DN_HEADS + h, ri)) for ri, h in heads]
    v = [conv_silu(2 * DN_HEADS + h, ri) for ri, h in heads]
    beta = [beta_all[ri][:, h:h + 1] for ri, h in heads]
    gc = [gc_all[ri][:, DN_HEADS + h:DN_HEADS + h + 1] for ri, h in heads]
    kb = _each(lambda x, b: x * b, k, beta)
    vb = _each(lambda x, b: x * b, v, beta)
    e_gc = _each(jnp.exp, gc)
    qe = _each(lambda x, e: x * e, q, e_gc)
    g_last = [x[CH - 1:CH, :] for x in gc]

    def decay_mask(item):
        n, ri, hp = item
        col = jnp.where(lo, gc[n * DN_HEADS + 2 * hp], gc[n * DN_HEADS + 2 * hp + 1])
        r1 = DN_HEADS + 2 * hp
        rowv = jnp.where(lo_row, gc_t[ri][r1:r1 + 1, :], gc_t[ri][r1 + 1:r1 + 2, :])
        return jnp.where(incl, jnp.exp(col - rowv), 0.0)

    decay = _each(decay_mask, pairs)

    def diag2(a, b):
        return jnp.concatenate([jnp.concatenate([a, zero], axis=1),
                                jnp.concatenate([zero, b], axis=1)], axis=0)

    scores = _each(
        lambda kb1, kb2, q1, q2, k1, k2: _mm_nt(
            jnp.concatenate([jnp.concatenate([kb1, kb2], axis=1),
                             jnp.concatenate([q1, q2], axis=1)], axis=0), diag2(k1, k2)),
        first(kb), second(kb), first(q), second(q), first(k), second(k))
    m_mat = _each(lambda s, d: jnp.where(strict, s[:CH] * d, 0.0), scores, decay)
    attn = _each(lambda s, d: jnp.where(incl, s[CH:] * d, 0.0), scores, decay)
    t_inv = _tri_inverse_each(m_mat)

    kbe = _each(lambda x, e: x * e, kb, e_gc)
    uw = _each(
        lambda t, vb1, vb2, kbe1, kbe2: _mm(t, jnp.concatenate([
            jnp.concatenate([vb1, kbe1, zero, zero], axis=1),
            jnp.concatenate([zero, zero, vb2, kbe2], axis=1)], axis=0)),
        t_inv, first(vb), second(vb), first(kbe), second(kbe))

    s_old = [s_ref[ri, h] for ri, h in heads]
    u_h = [uw[i // 2][:, (i % 2) * 2 * LANE:(i % 2) * 2 * LANE + LANE] for i in range(len(heads))]
    w_h = [uw[i // 2][:, (i % 2) * 2 * LANE + LANE:(i % 2 + 1) * 2 * LANE]
           for i in range(len(heads))]
    ws = _each(lambda w, qe_, s: _mm(jnp.concatenate([w, qe_], axis=0), s), w_h, qe, s_old)
    v_new = _each(lambda u_, ws_: u_ - ws_[:CH], u_h, ws)
    o2 = _each(lambda a, vn1, vn2: _mm(a, diag2(vn1, vn2)),
               attn, first(v_new), second(v_new))

    k_dec = _each(lambda x, gl, g: x * jnp.exp(gl - g), k, g_last, gc)
    s_add = _each(lambda k1, k2, vn1, vn2: _mm_tn(diag2(k1, k2), jnp.concatenate([vn1, vn2], axis=0)),
                  first(k_dec), second(k_dec), first(v_new), second(v_new))
    for i, (ri, h) in enumerate(heads):
        s_ref[ri, h] = (s_old[i] * jnp.exp(g_last[i])
                        + s_add[i // 2][(i % 2) * LANE:(i % 2 + 1) * LANE])
        o = ws[i][CH:] + o2[i // 2][:, (i % 2) * LANE:(i % 2 + 1) * LANE]
        o = o * lax.rsqrt(jnp.mean(o * o, axis=-1, keepdims=True) + NORM_EPS) * nw_ref[...]
        zg = p_ref[3 * DN_HEADS + h, ri]
        y_ref[h, ri] = (o * zg * _sigmoid(zg)).astype(y_ref.dtype)


def _gdn(p, prm, carry_in, state_in, rows):
    _, batch, seq, _ = p.shape
    n_chunks = seq // CH
    full = lambda shape: pl.BlockSpec(shape, lambda b, c: (0,) * len(shape))
    n_carry = 3 * DN_HEADS
    return pl.pallas_call(
        functools.partial(_gdn_kernel, n_chunks=n_chunks, rows=rows),
        grid=(batch // rows, n_chunks),
        in_specs=[
            pl.BlockSpec((DN_PGROUPS, rows, CH, LANE), lambda b, c: (1, b, c, 0)),
            full((CONV_W, n_carry, 1, LANE)),
            full((1, LANE)), full((1, LANE)), full((1, LANE)),
            full((n_carry, SUBLANE, LANE)),
            full((DN_HEADS, DN_HEAD, DN_HEAD)),
        ],
        out_specs=[
            pl.BlockSpec((DN_HEADS, rows, CH, LANE), lambda b, c: (0, b, c, 0)),
            full((n_carry, SUBLANE, LANE)),
            full((DN_HEADS, DN_HEAD, DN_HEAD)),
        ],
        out_shape=[
            jax.ShapeDtypeStruct((DN_HEADS, batch, seq, LANE), BF16),
            jax.ShapeDtypeStruct((n_carry, SUBLANE, LANE), F32),
            jax.ShapeDtypeStruct((DN_HEADS, DN_HEAD, DN_HEAD), F32),
        ],
        scratch_shapes=[
            pltpu.VMEM((rows, DN_HEADS, DN_HEAD, DN_HEAD), F32),
            pltpu.VMEM((rows, n_carry, SUBLANE, LANE), F32),
        ],
        compiler_params=pltpu.CompilerParams(
            dimension_semantics=("arbitrary", "arbitrary"), vmem_limit_bytes=VMEM_LIMIT),
        name="gated_delta_mix",
    )(p, prm["conv_w"], prm["a_log"], prm["dt_bias"], prm["norm_w"], carry_in, state_in)


def _outproj_kernel(ya_ref, yb_ref, x_ref, w_ref, fnw_ref, o_ref):
    y = jnp.concatenate([ya_ref[g] for g in range(RW_G)] + [yb_ref[h] for h in range(DN_HEADS)],
                        axis=1)
    hid = x_ref[...] + jnp.dot(y, w_ref[...], preferred_element_type=F32)
    ms = jnp.mean(hid * hid, axis=-1, keepdims=True)
    o_ref[...] = hid * lax.rsqrt(ms + NORM_EPS) * fnw_ref[...]


def _outproj(ya, yb, x2d, w_out, fnw, tm):
    m, d = x2d.shape
    return pl.pallas_call(
        _outproj_kernel,
        grid=(m // tm,),
        in_specs=[
            pl.BlockSpec((RW_G, tm, LANE), lambda i: (0, i, 0)),
            pl.BlockSpec((DN_HEADS, tm, LANE), lambda i: (0, i, 0)),
            pl.BlockSpec((tm, d), lambda i: (i, 0)),
            pl.BlockSpec((RW_WIDTH + DN_WIDTH, d), lambda i: (0, 0)),
            pl.BlockSpec((1, d), lambda i: (0, 0)),
        ],
        out_specs=pl.BlockSpec((tm, d), lambda i: (i, 0)),
        out_shape=jax.ShapeDtypeStruct((m, d), F32),
        compiler_params=pltpu.CompilerParams(
            dimension_semantics=("arbitrary",), vmem_limit_bytes=VMEM_LIMIT),
        name="outproj",
    )(ya, yb, x2d, w_out, fnw)


def _row_tile(m, cap):
    t = cap
    while m % t:
        t //= 2
    return t


def kernel(x, meta_tokens, norm_w, w_in, rw_shift_mu, rw_w0, rw_w2, rw_a0, rw_a2, rw_k_k, rw_k_a,
           rw_r_k, rw_gn_w, rw_gn_b, dn_conv_w, dn_A_log, dn_dt_bias, dn_norm_w, w_out,
           final_norm_w):
    batch, seq, d = x.shape
    assert seq % CH == 0 and norm_w.shape[0] == 1
    m = batch * seq

    tn = 768
    wi = w_in[0]
    c_b = 3 * RW_WIDTH + 2 * RW_LORA + RW_WIDTH + 3 * DN_WIDTH
    c_z = c_b + 2 * DN_HEADS
    c_tail = (P_GROUPS * LANE // tn - 2) * tn
    assert c_tail <= c_b and c_tail % LANE == 0
    w_tail = jnp.concatenate([
        wi[:, c_tail:c_b], wi[:, c_z:], wi[:, c_b:c_z],
        jnp.zeros((d, LANE - 2 * DN_HEADS), wi.dtype)], axis=1).astype(BF16)
    grp = lambda t: t.reshape(-1, 1, LANE)
    zeros_l = jnp.zeros((RW_LORA, RW_WIDTH), F32)
    rw_prm = {
        "mu": grp(rw_shift_mu[0]),
        "w0": grp(rw_w0[0]), "a0": grp(rw_a0[0]), "k_k": grp(rw_k_k[0]), "k_a": grp(rw_k_a[0]),
        "r_k": grp(rw_r_k[0]), "gn_w": grp(rw_gn_w[0]), "gn_b": grp(rw_gn_b[0]),
        "w2a2": jnp.concatenate([jnp.concatenate([rw_w2[0], zeros_l], axis=1),
                                 jnp.concatenate([zeros_l, rw_a2[0]], axis=1)], axis=0).astype(BF16),
    }
    lane_vec = lambda t: jnp.zeros((1, LANE), F32).at[0, DN_HEADS:2 * DN_HEADS].set(t)
    dn_prm = {
        "conv_w": dn_conv_w[0].reshape(CONV_W, 3 * DN_HEADS, 1, LANE),
        "a_log": lane_vec(dn_A_log[0]), "dt_bias": lane_vec(dn_dt_bias[0]),
        "norm_w": dn_norm_w[0].reshape(1, LANE),
    }
    nw = norm_w[0].reshape(1, d)

    meta_rows = jnp.concatenate([jnp.zeros((CH - N_META, d), x.dtype), meta_tokens.astype(x.dtype)],
                                axis=0)
    p_meta = _inproj(meta_rows, nw, wi, w_tail, CH, tn).reshape(P_GROUPS, 1, CH, LANE)
    zc = lambda n: jnp.zeros((n, SUBLANE, LANE), F32)
    _, rw_carry, rw_state = _rwkv(p_meta, rw_prm, zc(3 * RW_G + 1),
                                  jnp.zeros((RW_G, LANE, LANE), F32), 1)
    _, dn_carry, dn_state = _gdn(p_meta, dn_prm, zc(3 * DN_HEADS),
                                 jnp.zeros((DN_HEADS, DN_HEAD, DN_HEAD), F32), 1)

    x2d = x.reshape(m, d)
    p = _inproj(x2d, nw, wi, w_tail, _row_tile(m, 1024), tn).reshape(P_GROUPS, batch, seq, LANE)
    ya, _, _ = _rwkv(p, rw_prm, rw_carry, rw_state, 4 if batch % 4 == 0 else 1)
    yb, _, _ = _gdn(p, dn_prm, dn_carry, dn_state, 4 if batch % 4 == 0 else 1)
    out = _outproj(ya.reshape(RW_G, m, LANE), yb.reshape(DN_HEADS, m, LANE), x2d,
                   w_out[0].astype(BF16), final_norm_w.reshape(1, d), _row_tile(m, 512))
    return out.reshape(batch, seq, d)
```

```python
import functools

import jax
import jax.numpy as jnp
from jax import lax
from jax.experimental import pallas as pl
from jax.experimental.pallas import tpu as pltpu

F32 = jnp.float32
BF16 = jnp.bfloat16

LANE = 128
SUBLANE = 8
CH = 64
N_META = 16
RW_WIDTH = 1024
RW_HEAD = 64
RW_LORA = 64
DN_WIDTH = 1024
DN_HEAD = 128
DN_HEADS = DN_WIDTH // DN_HEAD
CONV_W = 4
NORM_EPS = 1e-6
L2_EPS = 1e-6
RW_GN_EPS = 64e-5
RW_G = RW_WIDTH // LANE
RW_PGROUPS = 4 * RW_G + 1
DN_PGROUPS = 4 * DN_HEADS + 1
P_GROUPS = RW_PGROUPS + DN_PGROUPS
VMEM_LIMIT = 56 * 1024 * 1024


def _mm(a, b):
    return jnp.dot(a.astype(BF16), b.astype(BF16), preferred_element_type=F32)


def _mm_nt(a, b):
    return lax.dot_general(a.astype(BF16), b.astype(BF16), (((1,), (1,)), ((), ())),
                           preferred_element_type=F32)


def _mm_tn(a, b):
    return lax.dot_general(a.astype(BF16), b.astype(BF16), (((0,), (0,)), ((), ())),
                           preferred_element_type=F32)


def _each(fn, *lists):
    return [fn(*args) for args in zip(*lists)]


def _cumsum_rows(tri, x):
    hi = x.astype(BF16)
    lo = (x - hi.astype(F32)).astype(BF16)
    dot = lambda h: jnp.dot(tri, h, preferred_element_type=F32)
    return dot(hi) + dot(lo)


def _sigmoid(z):
    return 0.5 * jnp.tanh(0.5 * z) + 0.5


def _shift_rows(x, tail, back):
    x3 = x.reshape(CH // SUBLANE, SUBLANE, LANE)
    rot = pltpu.roll(x3, back, 1)
    prev = jnp.concatenate([pltpu.roll(tail, back, 0)[None], rot[:-1]], axis=0)
    sub = lax.broadcasted_iota(jnp.int32, x3.shape, 1)
    return jnp.where(sub < back, prev, rot).reshape(CH, LANE)


def _pair_time_index():
    t = lax.broadcasted_iota(jnp.int32, (CH, 2 * CH), 0)
    lane = lax.broadcasted_iota(jnp.int32, (CH, 2 * CH), 1)
    return t, lane & (CH - 1), lane < CH


def _block_diag(x, first):
    xb = x.astype(BF16)
    zero = jnp.zeros_like(xb)
    return jnp.concatenate([jnp.where(first, xb, zero), jnp.where(first, zero, xb)], axis=0)


def _tri_inverse_each(mats):
    i, j, first = _pair_time_index()
    eye = (i == j).astype(BF16)
    zero = jnp.zeros_like(eye)

    def off(m):
        return ((i & -(2 * m)) == (j & -(2 * m))) & ((i & m) != 0) & ((j & m) == 0)

    neg = [(-a).astype(BF16) for a in mats]
    ts = [jnp.where(off(1), na, eye) for na in neg]
    m = 2
    while m < CH:
        mask = off(m)
        prods = _each(lambda na, t: _mm(jnp.where(mask, na, zero), _block_diag(t, first)), neg, ts)
        corr = _each(lambda t, p: _mm(t, _block_diag(p, first)), ts, prods)
        ts = _each(lambda t, c: jnp.where(mask, c.astype(BF16), t), ts, corr)
        m *= 2
    return ts


def _same_head_mask():
    i2 = lax.broadcasted_iota(jnp.int32, (2 * CH, 2 * CH), 0)
    j2 = lax.broadcasted_iota(jnp.int32, (2 * CH, 2 * CH), 1)
    return (i2 >= CH) == (j2 >= CH)


def _tri_ones():
    ti = lax.broadcasted_iota(jnp.int32, (CH, CH), 0)
    tj = lax.broadcasted_iota(jnp.int32, (CH, CH), 1)
    return (ti >= tj).astype(BF16)


def _inproj_kernel(x_ref, nw_ref, wa_ref, wb_ref, o_ref, u_ref, *, n_sub, n_main):
    j = pl.program_id(1)

    @pl.when(j == 0)
    def _():
        x = x_ref[...]
        ms = jnp.mean(x * x, axis=-1, keepdims=True)
        u_ref[...] = (x * lax.rsqrt(ms + NORM_EPS) * nw_ref[...]).astype(BF16)

    def project(w_t):
        acc = lax.dot_general(u_ref[...], w_t, (((1,), (1,)), ((), ())),
                              preferred_element_type=F32)
        for s in range(n_sub):
            o_ref[s] = acc[:, s * LANE:(s + 1) * LANE]

    @pl.when(j < n_main)
    def _():
        project(wa_ref[...])

    @pl.when(j >= n_main)
    def _():
        project(wb_ref[...])


def _inproj(x2d, norm_w, w_main, w_tail, tm, tn):
    m, d = x2d.shape
    n_tail = w_tail.shape[0] // tn
    n_main = P_GROUPS * LANE // tn - n_tail
    n_sub = tn // LANE
    return pl.pallas_call(
        functools.partial(_inproj_kernel, n_sub=n_sub, n_main=n_main),
        grid=(m // tm, n_main + n_tail),
        in_specs=[
            pl.BlockSpec((tm, d), lambda i, j: (i, 0)),
            pl.BlockSpec((1, d), lambda i, j: (0, 0)),
            pl.BlockSpec((tn, d), lambda i, j: (jnp.minimum(j, n_main - 1), 0)),
            pl.BlockSpec((tn, d), lambda i, j: (jnp.maximum(j - n_main, 0), 0)),
        ],
        out_specs=pl.BlockSpec((n_sub, tm, LANE), lambda i, j: (j, i, 0)),
        out_shape=jax.ShapeDtypeStruct((P_GROUPS, m, LANE), F32),
        scratch_shapes=[pltpu.VMEM((tm, d), BF16)],
        compiler_params=pltpu.CompilerParams(
            dimension_semantics=("arbitrary", "arbitrary"), vmem_limit_bytes=VMEM_LIMIT),
        name="inproj",
    )(x2d, norm_w, w_main, w_tail)


def _rwkv_kernel(p_ref, mu_ref, w0_ref, a0_ref, kk_ref, ka_ref, rk_ref, gnw_ref, gnb_ref,
                 w2a2_ref, cin_ref, s0_ref, y_ref, cout_ref, sout_ref, s_ref, c_ref,
                 *, n_chunks, rows):
    c = pl.program_id(1)

    @pl.when(c == 0)
    def _():
        for ri in range(rows):
            s_ref[ri] = s0_ref[...]
            c_ref[ri] = cin_ref[...]

    _rwkv_rows(p_ref, mu_ref, w0_ref, a0_ref, kk_ref, ka_ref, rk_ref, gnw_ref, gnb_ref,
               w2a2_ref, y_ref, s_ref, c_ref, row_ids=tuple(range(rows)))

    for ri in range(rows):
        c_ref[ri] = p_ref[0:3 * RW_G + 1, ri, CH - SUBLANE:CH, :]

    @pl.when(c == n_chunks - 1)
    def _():
        cout_ref[...] = c_ref[rows - 1]
        sout_ref[...] = s_ref[rows - 1]


def _rwkv_rows(p_ref, mu_ref, w0_ref, a0_ref, kk_ref, ka_ref, rk_ref, gnw_ref, gnb_ref,
               w2a2_ref, y_ref, s_ref, c_ref, *, row_ids):
    t_i, s_i, lo = _pair_time_index()
    strict = t_i > s_i
    incl = t_i >= s_i
    same_head = _same_head_mask()
    tri = _tri_ones()
    items = [(ri, g) for ri in row_ids for g in range(RW_G)]
    bd = lambda x: _block_diag(x, lo)

    def seg_sum(x):
        s_lo = jnp.sum(jnp.where(lo, x, 0.0), axis=-1, keepdims=True)
        s_hi = jnp.sum(jnp.where(lo, 0.0, x), axis=-1, keepdims=True)
        return jnp.where(lo, s_lo, s_hi)

    def shifted(group, ri, mu_idx, tail_idx):
        x = p_ref[group, ri]
        return x + (_shift_rows(x, c_ref[ri, tail_idx], 1) - x) * mu_ref[mu_idx]

    loras = {}
    for ri in row_ids:
        lg = shifted(3 * RW_G, ri, 3 * RW_G, 3 * RW_G)
        xl = jnp.where(lo, jnp.tanh(lg), lg)
        loras[ri] = jnp.dot(xl.astype(BF16), w2a2_ref[...], preferred_element_type=F32)

    r_s = [shifted(g, ri, g, g) for ri, g in items]
    k_s = [shifted(RW_G + g, ri, RW_G + g, RW_G + g) for ri, g in items]
    v_s = [shifted(2 * RW_G + g, ri, 2 * RW_G + g, 2 * RW_G + g) for ri, g in items]

    logw = [-jnp.exp(-0.5) * _sigmoid(w0_ref[g] + loras[ri][:, g * LANE:(g + 1) * LANE])
            for ri, g in items]
    a_s = [_sigmoid(a0_ref[g] + loras[ri][:, RW_WIDTH + g * LANE:RW_WIDTH + (g + 1) * LANE])
           for ri, g in items]

    def removal_key(k, item):
        kx = k * kk_ref[item[1]]
        return kx * lax.rsqrt(seg_sum(kx * kx) + L2_EPS)

    kk_s = _each(removal_key, k_s, items)
    kmod = _each(lambda k, a, item: k * (1.0 + (a - 1.0) * ka_ref[item[1]]), k_s, a_s, items)
    bb = _each(lambda kk, a: kk * a, kk_s, a_s)

    gam = []
    for n in range(len(row_ids)):
        wide = _cumsum_rows(tri, jnp.concatenate(logw[n * RW_G:(n + 1) * RW_G], axis=1))
        gam += [wide[:, g * LANE:(g + 1) * LANE] for g in range(RW_G)]

    e_in = _each(jnp.exp, gam)
    e_inv = _each(lambda x: jnp.exp(-x), gam)
    e_rem = _each(lambda x: jnp.exp(x[CH - 1:CH, :] - x), gam)
    rt = _each(lambda r, e: r * e, r_s, e_in)
    kt = _each(lambda kk, x, lw: kk * jnp.exp(x - lw), kk_s, gam, logw)
    b_inv = _each(lambda b, e: b * e, bb, e_inv)
    k_inv = _each(lambda k, e: k * e, kmod, e_inv)

    a_all = _each(
        lambda kt_, rt_, b_, k_: _mm_nt(jnp.concatenate([kt_, rt_], axis=0),
                                        jnp.concatenate([bd(b_), bd(k_)], axis=0)),
        kt, rt, b_inv, k_inv)
    a_kb = [jnp.where(strict, a[:CH, :LANE], 0.0) for a in a_all]
    a_kk = [jnp.where(strict, a[:CH, LANE:], 0.0) for a in a_all]
    a_rb = [jnp.where(incl, a[CH:, :LANE], 0.0) for a in a_all]
    a_rk = [jnp.where(incl, a[CH:, LANE:], 0.0) for a in a_all]
    t_inv = _tri_inverse_each(a_kb)

    av = _each(lambda kk_, rk_, v: _mm(jnp.concatenate([kk_, rk_], axis=0), bd(v)),
               a_kk, a_rk, v_s)
    tx = _each(lambda t, kt_, av_: _mm(t, jnp.concatenate([bd(kt_), bd(av_[:CH])], axis=1)),
               t_inv, kt, av)
    w_nat = [x[:, :LANE] for x in tx]
    u0 = [x[:, LANE:] for x in tx]
    y0 = [x[CH:] for x in av]

    s_old = [s_ref[ri, g] for ri, g in items]
    wh = _each(lambda w, r, s: _mm_nt(jnp.concatenate([w, r], axis=0), s), w_nat, rt, s_old)
    u = _each(lambda wh_, u0_: wh_[:CH] + u0_, wh, u0)
    arbu = _each(lambda a, u_: _mm(a, bd(u_)), a_rb, u)
    y = _each(lambda wh_, y0_, x: wh_[CH:] + y0_ - x, wh, y0, arbu)
    s_add = _each(
        lambda v, u_, k, b, e: _mm_tn(jnp.concatenate([v, u_], axis=0),
                                      jnp.concatenate([k * e, -(b * e)], axis=0)),
        v_s, u, kmod, bb, e_rem)
    for (ri, g), s, e, add in zip(items, s_old, e_in, s_add):
        s_ref[ri, g] = s * e[CH - 1:CH, :] + jnp.where(same_head, add, 0.0)

    def finish(y_, r, k, v, item):
        ri, g = item
        mean = seg_sum(y_) * (1.0 / RW_HEAD)
        d = y_ - mean
        var = seg_sum(d * d) * (1.0 / RW_HEAD)
        yn = d * lax.rsqrt(var + RW_GN_EPS) * gnw_ref[g] + gnb_ref[g]
        yn = yn + seg_sum(r * k * rk_ref[g]) * v
        gate = p_ref[3 * RW_G + 1 + g, ri]
        y_ref[g, ri] = (yn * gate * _sigmoid(gate)).astype(y_ref.dtype)

    _each(finish, y, r_s, kmod, v_s, items)


def _rwkv(p, prm, carry_in, state_in, rows):
    _, batch, seq, _ = p.shape
    n_chunks = seq // CH
    full = lambda shape: pl.BlockSpec(shape, lambda b, c: (0,) * len(shape))
    n_carry = 3 * RW_G + 1
    return pl.pallas_call(
        functools.partial(_rwkv_kernel, n_chunks=n_chunks, rows=rows),
        grid=(batch // rows, n_chunks),
        in_specs=[
            pl.BlockSpec((RW_PGROUPS, rows, CH, LANE), lambda b, c: (0, b, c, 0)),
            full((n_carry, 1, LANE)),
            full((RW_G, 1, LANE)), full((RW_G, 1, LANE)), full((RW_G, 1, LANE)),
            full((RW_G, 1, LANE)), full((RW_G, 1, LANE)), full((RW_G, 1, LANE)),
            full((RW_G, 1, LANE)),
            full((2 * RW_LORA, 2 * RW_WIDTH)),
            full((n_carry, SUBLANE, LANE)),
            full((RW_G, LANE, LANE)),
        ],
        out_specs=[
            pl.BlockSpec((RW_G, rows, CH, LANE), lambda b, c: (0, b, c, 0)),
            full((n_carry, SUBLANE, LANE)),
            full((RW_G, LANE, LANE)),
        ],
        out_shape=[
            jax.ShapeDtypeStruct((RW_G, batch, seq, LANE), BF16),
            jax.ShapeDtypeStruct((n_carry, SUBLANE, LANE), F32),
            jax.ShapeDtypeStruct((RW_G, LANE, LANE), F32),
        ],
        scratch_shapes=[
            pltpu.VMEM((rows, RW_G, LANE, LANE), F32),
            pltpu.VMEM((rows, n_carry, SUBLANE, LANE), F32),
        ],
        compiler_params=pltpu.CompilerParams(
            dimension_semantics=("arbitrary", "arbitrary"), vmem_limit_bytes=VMEM_LIMIT),
        name="rwkv7_mix",
    )(p, prm["mu"], prm["w0"], prm["a0"], prm["k_k"], prm["k_a"], prm["r_k"], prm["gn_w"],
      prm["gn_b"], prm["w2a2"], carry_in, state_in)


def _gdn_kernel(p_ref, cw_ref, alog_ref, dtb_ref, nw_ref, cin_ref, s0_ref,
                y_ref, cout_ref, sout_ref, s_ref, c_ref, *, n_chunks, rows):
    c = pl.program_id(1)

    @pl.when(c == 0)
    def _():
        for ri in range(rows):
            s_ref[ri] = s0_ref[...]
            c_ref[ri] = cin_ref[...]

    _gdn_rows(p_ref, cw_ref, alog_ref, dtb_ref, nw_ref, y_ref, s_ref, c_ref,
              row_ids=tuple(range(rows)))

    for ri in range(rows):
        c_ref[ri] = p_ref[0:3 * DN_HEADS, ri, CH - SUBLANE:CH, :]

    @pl.when(c == n_chunks - 1)
    def _():
        cout_ref[...] = c_ref[rows - 1]
        sout_ref[...] = s_ref[rows - 1]


def _gdn_rows(p_ref, cw_ref, alog_ref, dtb_ref, nw_ref, y_ref, s_ref, c_ref, *, row_ids):
    t_i, s_i, lo = _pair_time_index()
    strict = t_i > s_i
    incl = t_i >= s_i
    lo_row = lo[:1]
    tri = _tri_ones()
    zero = jnp.zeros((CH, LANE), F32)
    heads = [(ri, h) for ri in row_ids for h in range(DN_HEADS)]
    pairs = [(n, ri, hp) for n, ri in enumerate(row_ids) for hp in range(DN_HEADS // 2)]
    first = lambda xs: xs[0::2]
    second = lambda xs: xs[1::2]

    def conv_silu(group, ri):
        x = p_ref[group, ri]
        tail = c_ref[ri, group]
        acc = x * cw_ref[CONV_W - 1, group]
        for back in range(1, CONV_W):
            acc = acc + _shift_rows(x, tail, back) * cw_ref[CONV_W - 1 - back, group]
        return acc * _sigmoid(acc)

    def l2n(x):
        return x * lax.rsqrt(jnp.sum(x * x, axis=-1, keepdims=True) + L2_EPS)

    beta_all, gc_all, gc_t = {}, {}, {}
    for ri in row_ids:
        ba = p_ref[4 * DN_HEADS, ri]
        beta_all[ri] = _sigmoid(ba)
        z = ba + dtb_ref[...]
        softplus = jnp.maximum(z, 0.0) + jnp.log1p(jnp.exp(-jnp.abs(z)))
        gc_all[ri] = _cumsum_rows(tri, -jnp.exp(alog_ref[...]) * softplus)
        gc_t[ri] = jnp.concatenate([gc_all[ri], gc_all[ri]], axis=0).T

    q = [l2n(conv_silu(h, ri)) * (DN_HEAD ** -0.5) for ri, h in heads]
    k = [l2n(conv_silu(DN_HEADS + h, ri)) for ri, h in heads]
    v = [conv_silu(2 * DN_HEADS + h, ri) for ri, h in heads]
    beta = [beta_all[ri][:, h:h + 1] for ri, h in heads]
    gc = [gc_all[ri][:, DN_HEADS + h:DN_HEADS + h + 1] for ri, h in heads]
    kb = _each(lambda x, b: x * b, k, beta)
    vb = _each(lambda x, b: x * b, v, beta)
    e_gc = _each(jnp.exp, gc)
    qe = _each(lambda x, e: x * e, q, e_gc)
    g_last = [x[CH - 1:CH, :] for x in gc]

    def decay_mask(item):
        n, ri, hp = item
        col = jnp.where(lo, gc[n * DN_HEADS + 2 * hp], gc[n * DN_HEADS + 2 * hp + 1])
        r1 = DN_HEADS + 2 * hp
        rowv = jnp.where(lo_row, gc_t[ri][r1:r1 + 1, :], gc_t[ri][r1 + 1:r1 + 2, :])
        return jnp.where(incl, jnp.exp(col - rowv), 0.0)

    decay = _each(decay_mask, pairs)

    def diag2(a, b):
        return jnp.concatenate([jnp.concatenate([a, zero], axis=1),
                                jnp.concatenate([zero, b], axis=1)], axis=0)

    scores = _each(
        lambda kb1, kb2, q1, q2, k1, k2: _mm_nt(
            jnp.concatenate([jnp.concatenate([kb1, kb2], axis=1),
                             jnp.concatenate([q1, q2], axis=1)], axis=0), diag2(k1, k2)),
        first(kb), second(kb), first(q), second(q), first(k), second(k))
    m_mat = _each(lambda s, d: jnp.where(strict, s[:CH] * d, 0.0), scores, decay)
    attn = _each(lambda s, d: jnp.where(incl, s[CH:] * d, 0.0), scores, decay)
    t_inv = _tri_inverse_each(m_mat)

    kbe = _each(lambda x, e: x * e, kb, e_gc)
    uw = _each(
        lambda t, vb1, vb2, kbe1, kbe2: _mm(t, jnp.concatenate([
            jnp.concatenate([vb1, kbe1, zero, zero], axis=1),
            jnp.concatenate([zero, zero, vb2, kbe2], axis=1)], axis=0)),
        t_inv, first(vb), second(vb), first(kbe), second(kbe))

    s_old = [s_ref[ri, h] for ri, h in heads]
    u_h = [uw[i // 2][:, (i % 2) * 2 * LANE:(i % 2) * 2 * LANE + LANE] for i in range(len(heads))]
    w_h = [uw[i // 2][:, (i % 2) * 2 * LANE + LANE:(i % 2 + 1) * 2 * LANE]
           for i in range(len(heads))]
    ws = _each(lambda w, qe_, s: _mm(jnp.concatenate([w, qe_], axis=0), s), w_h, qe, s_old)
    v_new = _each(lambda u_, ws_: u_ - ws_[:CH], u_h, ws)
    o2 = _each(lambda a, vn1, vn2: _mm(a, diag2(vn1, vn2)),
               attn, first(v_new), second(v_new))

    k_dec = _each(lambda x, gl, g: x * jnp.exp(gl - g), k, g_last, gc)
    s_add = _each(lambda k1, k2, vn1, vn2: _mm_tn(diag2(k1, k2), jnp.concatenate([vn1, vn2], axis=0)),
                  first(k_dec), second(k_dec), first(v_new), second(v_new))
    for i, (ri, h) in enumerate(heads):
        s_ref[ri, h] = (s_old[i] * jnp.exp(g_last[i])
                        + s_add[i // 2][(i % 2) * LANE:(i % 2 + 1) * LANE])
        o = ws[i][CH:] + o2[i // 2][:, (i % 2) * LANE:(i % 2 + 1) * LANE]
        o = o * lax.rsqrt(jnp.mean(o * o, axis=-1, keepdims=True) + NORM_EPS) * nw_ref[...]
        zg = p_ref[3 * DN_HEADS + h, ri]
        y_ref[h, ri] = (o * zg * _sigmoid(zg)).astype(y_ref.dtype)


def _gdn(p, prm, carry_in, state_in, rows):
    _, batch, seq, _ = p.shape
    n_chunks = seq // CH
    full = lambda shape: pl.BlockSpec(shape, lambda b, c: (0,) * len(shape))
    n_carry = 3 * DN_HEADS
    return pl.pallas_call(
        functools.partial(_gdn_kernel, n_chunks=n_chunks, rows=rows),
        grid=(batch // rows, n_chunks),
        in_specs=[
            pl.BlockSpec((DN_PGROUPS, rows, CH, LANE), lambda b, c: (1, b, c, 0)),
            full((CONV_W, n_carry, 1, LANE)),
            full((1, LANE)), full((1, LANE)), full((1, LANE)),
            full((n_carry, SUBLANE, LANE)),
            full((DN_HEADS, DN_HEAD, DN_HEAD)),
        ],
        out_specs=[
            pl.BlockSpec((DN_HEADS, rows, CH, LANE), lambda b, c: (0, b, c, 0)),
            full((n_carry, SUBLANE, LANE)),
            full((DN_HEADS, DN_HEAD, DN_HEAD)),
        ],
        out_shape=[
            jax.ShapeDtypeStruct((DN_HEADS, batch, seq, LANE), BF16),
            jax.ShapeDtypeStruct((n_carry, SUBLANE, LANE), F32),
            jax.ShapeDtypeStruct((DN_HEADS, DN_HEAD, DN_HEAD), F32),
        ],
        scratch_shapes=[
            pltpu.VMEM((rows, DN_HEADS, DN_HEAD, DN_HEAD), F32),
            pltpu.VMEM((rows, n_carry, SUBLANE, LANE), F32),
        ],
        compiler_params=pltpu.CompilerParams(
            dimension_semantics=("arbitrary", "arbitrary"), vmem_limit_bytes=VMEM_LIMIT),
        name="gated_delta_mix",
    )(p, prm["conv_w"], prm["a_log"], prm["dt_bias"], prm["norm_w"], carry_in, state_in)


def _outproj_kernel(ya_ref, yb_ref, x_ref, w_ref, fnw_ref, o_ref):
    y = jnp.concatenate([ya_ref[g] for g in range(RW_G)] + [yb_ref[h] for h in range(DN_HEADS)],
                        axis=1)
    hid = x_ref[...] + jnp.dot(y, w_ref[...], preferred_element_type=F32)
    ms = jnp.mean(hid * hid, axis=-1, keepdims=True)
    o_ref[...] = hid * lax.rsqrt(ms + NORM_EPS) * fnw_ref[...]


def _outproj(ya, yb, x2d, w_out, fnw, tm):
    m, d = x2d.shape
    return pl.pallas_call(
        _outproj_kernel,
        grid=(m // tm,),
        in_specs=[
            pl.BlockSpec((RW_G, tm, LANE), lambda i: (0, i, 0)),
            pl.BlockSpec((DN_HEADS, tm, LANE), lambda i: (0, i, 0)),
            pl.BlockSpec((tm, d), lambda i: (i, 0)),
            pl.BlockSpec((RW_WIDTH + DN_WIDTH, d), lambda i: (0, 0)),
            pl.BlockSpec((1, d), lambda i: (0, 0)),
        ],
        out_specs=pl.BlockSpec((tm, d), lambda i: (i, 0)),
        out_shape=jax.ShapeDtypeStruct((m, d), F32),
        compiler_params=pltpu.CompilerParams(
            dimension_semantics=("arbitrary",), vmem_limit_bytes=VMEM_LIMIT),
        name="outproj",
    )(ya, yb, x2d, w_out, fnw)


def _row_tile(m, cap):
    t = cap
    while m % t:
        t //= 2
    return t


def kernel(x, meta_tokens, norm_w, w_in, rw_shift_mu, rw_w0, rw_w2, rw_a0, rw_a2, rw_k_k, rw_k_a,
           rw_r_k, rw_gn_w, rw_gn_b, dn_conv_w, dn_A_log, dn_dt_bias, dn_norm_w, w_out,
           final_norm_w):
    batch, seq, d = x.shape
    assert seq % CH == 0 and norm_w.shape[0] == 1
    m = batch * seq

    tn = 768
    wi = w_in[0].T.astype(BF16)
    c_b = 3 * RW_WIDTH + 2 * RW_LORA + RW_WIDTH + 3 * DN_WIDTH
    c_z = c_b + 2 * DN_HEADS
    c_tail = (P_GROUPS * LANE // tn - 2) * tn
    assert c_tail <= c_b and c_tail % LANE == 0
    w_tail = jnp.concatenate([
        wi[c_tail:c_b], wi[c_z:], wi[c_b:c_z],
        jnp.zeros((LANE - 2 * DN_HEADS, d), wi.dtype)], axis=0)
    grp = lambda t: t.reshape(-1, 1, LANE)
    zeros_l = jnp.zeros((RW_LORA, RW_WIDTH), F32)
    rw_prm = {
        "mu": grp(rw_shift_mu[0]),
        "w0": grp(rw_w0[0]), "a0": grp(rw_a0[0]), "k_k": grp(rw_k_k[0]), "k_a": grp(rw_k_a[0]),
        "r_k": grp(rw_r_k[0]), "gn_w": grp(rw_gn_w[0]), "gn_b": grp(rw_gn_b[0]),
        "w2a2": jnp.concatenate([jnp.concatenate([rw_w2[0], zeros_l], axis=1),
                                 jnp.concatenate([zeros_l, rw_a2[0]], axis=1)], axis=0).astype(BF16),
    }
    lane_vec = lambda t: jnp.zeros((1, LANE), F32).at[0, DN_HEADS:2 * DN_HEADS].set(t)
    dn_prm = {
        "conv_w": dn_conv_w[0].reshape(CONV_W, 3 * DN_HEADS, 1, LANE),
        "a_log": lane_vec(dn_A_log[0]), "dt_bias": lane_vec(dn_dt_bias[0]),
        "norm_w": dn_norm_w[0].reshape(1, LANE),
    }
    nw = norm_w[0].reshape(1, d)

    meta_rows = jnp.concatenate([jnp.zeros((CH - N_META, d), x.dtype), meta_tokens.astype(x.dtype)],
                                axis=0)
    p_meta = _inproj(meta_rows, nw, wi, w_tail, CH, tn).reshape(P_GROUPS, 1, CH, LANE)
    zc = lambda n: jnp.zeros((n, SUBLANE, LANE), F32)
    _, rw_carry, rw_state = _rwkv(p_meta, rw_prm, zc(3 * RW_G + 1),
                                  jnp.zeros((RW_G, LANE, LANE), F32), 1)
    _, dn_carry, dn_state = _gdn(p_meta, dn_prm, zc(3 * DN_HEADS),
                                 jnp.zeros((DN_HEADS, DN_HEAD, DN_HEAD), F32), 1)

    x2d = x.reshape(m, d)
    p = _inproj(x2d, nw, wi, w_tail, _row_tile(m, 1024), tn).reshape(P_GROUPS, batch, seq, LANE)
    ya, _, _ = _rwkv(p, rw_prm, rw_carry, rw_state, 4 if batch % 4 == 0 else 1)
    yb, _, _ = _gdn(p, dn_prm, dn_carry, dn_state, 4 if batch % 4 == 0 else 1)
    out = _outproj(ya.reshape(RW_G, m, LANE), yb.reshape(DN_HEADS, m, LANE), x2d,
                   w_out[0].astype(BF16), final_norm_w.reshape(1, d), _row_tile(m, 512))
    return out.reshape(batch, seq, d)
```

```python
import functools

import jax
import jax.numpy as jnp
from jax import lax
from jax.experimental import pallas as pl
from jax.experimental.pallas import tpu as pltpu

F32 = jnp.float32
BF16 = jnp.bfloat16

LANE = 128
SUBLANE = 8
CH = 64
N_META = 16
RW_WIDTH = 1024
RW_HEAD = 64
RW_LORA = 64
DN_WIDTH = 1024
DN_HEAD = 128
DN_HEADS = DN_WIDTH // DN_HEAD
CONV_W = 4
NORM_EPS = 1e-6
L2_EPS = 1e-6
RW_GN_EPS = 64e-5
RW_G = RW_WIDTH // LANE
RW_PGROUPS = 4 * RW_G + 1
DN_PGROUPS = 4 * DN_HEADS + 1
P_GROUPS = RW_PGROUPS + DN_PGROUPS
VMEM_LIMIT = 56 * 1024 * 1024


def _mm(a, b):
    return jnp.dot(a.astype(BF16), b.astype(BF16), preferred_element_type=F32)


def _mm_nt(a, b):
    return lax.dot_general(a.astype(BF16), b.astype(BF16), (((1,), (1,)), ((), ())),
                           preferred_element_type=F32)


def _mm_tn(a, b):
    return lax.dot_general(a.astype(BF16), b.astype(BF16), (((0,), (0,)), ((), ())),
                           preferred_element_type=F32)


def _each(fn, *lists):
    return [fn(*args) for args in zip(*lists)]


def _cumsum_rows(tri, x):
    hi = x.astype(BF16)
    lo = (x - hi.astype(F32)).astype(BF16)
    dot = lambda h: jnp.dot(tri, h, preferred_element_type=F32)
    return dot(hi) + dot(lo)


def _sigmoid(z):
    return 0.5 * jnp.tanh(0.5 * z) + 0.5


def _silu(z):
    h = 0.5 * z
    return h + h * jnp.tanh(h)


def _pair_time_index():
    t = lax.broadcasted_iota(jnp.int32, (CH, 2 * CH), 0)
    lane = lax.broadcasted_iota(jnp.int32, (CH, 2 * CH), 1)
    return t, lane & (CH - 1), lane < CH


def _block_diag(x, first):
    xb = x.astype(BF16)
    zero = jnp.zeros_like(xb)
    return jnp.concatenate([jnp.where(first, xb, zero), jnp.where(first, zero, xb)], axis=0)


def _tri_inverse_each(mats):
    i, j, first = _pair_time_index()
    eye = (i == j).astype(BF16)
    zero = jnp.zeros_like(eye)

    def off(m):
        return ((i & -(2 * m)) == (j & -(2 * m))) & ((i & m) != 0) & ((j & m) == 0)

    neg = [(-a).astype(BF16) for a in mats]
    ts = [jnp.where(off(1), na, eye) for na in neg]
    m = 2
    while m < CH:
        mask = off(m)
        prods = _each(lambda na, t: _mm(jnp.where(mask, na, zero), _block_diag(t, first)), neg, ts)
        corr = _each(lambda t, p: _mm(t, _block_diag(p, first)), ts, prods)
        ts = _each(lambda t, c: jnp.where(mask, c.astype(BF16), t), ts, corr)
        m *= 2
    return ts


def _same_head_mask():
    i2 = lax.broadcasted_iota(jnp.int32, (2 * CH, 2 * CH), 0)
    j2 = lax.broadcasted_iota(jnp.int32, (2 * CH, 2 * CH), 1)
    return (i2 >= CH) == (j2 >= CH)


def _tri_ones():
    ti = lax.broadcasted_iota(jnp.int32, (CH, CH), 0)
    tj = lax.broadcasted_iota(jnp.int32, (CH, CH), 1)
    return (ti >= tj).astype(BF16)


def _inproj_kernel(x_ref, nw_ref, wa_ref, wb_ref, o_ref, u_ref, *, n_sub, n_main):
    j = pl.program_id(1)

    @pl.when(j == 0)
    def _():
        x = x_ref[...]
        ms = jnp.mean(x * x, axis=-1, keepdims=True)
        u_ref[...] = (x * lax.rsqrt(ms + NORM_EPS) * nw_ref[...]).astype(BF16)

    def project(w_t):
        acc = lax.dot_general(u_ref[...], w_t, (((1,), (1,)), ((), ())),
                              preferred_element_type=F32)
        for s in range(n_sub):
            o_ref[s] = acc[:, s * LANE:(s + 1) * LANE]

    @pl.when(j < n_main)
    def _():
        project(wa_ref[...])

    @pl.when(j >= n_main)
    def _():
        project(wb_ref[...])


def _inproj(x2d, norm_w, w_main, w_tail, tm, tn):
    m, d = x2d.shape
    n_tail = w_tail.shape[0] // tn
    n_main = P_GROUPS * LANE // tn - n_tail
    n_sub = tn // LANE
    return pl.pallas_call(
        functools.partial(_inproj_kernel, n_sub=n_sub, n_main=n_main),
        grid=(m // tm, n_main + n_tail),
        in_specs=[
            pl.BlockSpec((tm, d), lambda i, j: (i, 0)),
            pl.BlockSpec((1, d), lambda i, j: (0, 0)),
            pl.BlockSpec((tn, d), lambda i, j: (jnp.minimum(j, n_main - 1), 0)),
            pl.BlockSpec((tn, d), lambda i, j: (jnp.maximum(j - n_main, 0), 0)),
        ],
        out_specs=pl.BlockSpec((n_sub, tm, LANE), lambda i, j: (j, i, 0)),
        out_shape=jax.ShapeDtypeStruct((P_GROUPS, m, LANE), F32),
        scratch_shapes=[pltpu.VMEM((tm, d), BF16)],
        compiler_params=pltpu.CompilerParams(
            dimension_semantics=("arbitrary", "arbitrary"), vmem_limit_bytes=VMEM_LIMIT),
        name="inproj",
    )(x2d, norm_w, w_main, w_tail)


def _rwkv_kernel(p_ref, mu_ref, w0_ref, a0_ref, kk_ref, ka_ref, rk_ref, gnw_ref, gnb_ref,
                 w2a2_ref, cin_ref, s0_ref, y_ref, cout_ref, sout_ref, s_ref, c_ref,
                 *, n_chunks, rows):
    c = pl.program_id(1)

    @pl.when(c == 0)
    def _():
        for ri in range(rows):
            s_ref[ri] = s0_ref[...]
            c_ref[ri, :, 0:SUBLANE, :] = cin_ref[...]

    _rwkv_rows(p_ref, mu_ref, w0_ref, a0_ref, kk_ref, ka_ref, rk_ref, gnw_ref, gnb_ref,
               w2a2_ref, y_ref, s_ref, c_ref, row_ids=tuple(range(rows)))

    for ri in range(rows):
        c_ref[ri, :, 0:SUBLANE, :] = p_ref[0:3 * RW_G + 1, ri, CH - SUBLANE:CH, :]

    @pl.when(c == n_chunks - 1)
    def _():
        cout_ref[...] = c_ref[rows - 1, :, 0:SUBLANE, :]
        sout_ref[...] = s_ref[rows - 1]


def _rwkv_rows(p_ref, mu_ref, w0_ref, a0_ref, kk_ref, ka_ref, rk_ref, gnw_ref, gnb_ref,
               w2a2_ref, y_ref, s_ref, c_ref, *, row_ids):
    t_i, s_i, lo = _pair_time_index()
    strict = t_i > s_i
    incl = t_i >= s_i
    same_head = _same_head_mask()
    tri = _tri_ones()
    items = [(ri, g) for ri in row_ids for g in range(RW_G)]
    bd = lambda x: _block_diag(x, lo)

    def seg_sum(x):
        s_lo = jnp.sum(jnp.where(lo, x, 0.0), axis=-1, keepdims=True)
        s_hi = jnp.sum(jnp.where(lo, 0.0, x), axis=-1, keepdims=True)
        return jnp.where(lo, s_lo, s_hi)

    def shifted(group, ri, mu_idx, tail_idx):
        x = p_ref[group, ri]
        c_ref[ri, tail_idx, SUBLANE:, :] = x
        prev = c_ref[ri, tail_idx, SUBLANE - 1:SUBLANE - 1 + CH, :]
        return x + (prev - x) * mu_ref[mu_idx]

    loras = {}
    for ri in row_ids:
        lg = shifted(3 * RW_G, ri, 3 * RW_G, 3 * RW_G)
        xl = jnp.where(lo, jnp.tanh(lg), lg)
        loras[ri] = jnp.dot(xl.astype(BF16), w2a2_ref[...], preferred_element_type=F32)

    def mix_inputs(item):
        ri, g = item
        r = shifted(g, ri, g, g)
        k = shifted(RW_G + g, ri, RW_G + g, RW_G + g)
        v = shifted(2 * RW_G + g, ri, 2 * RW_G + g, 2 * RW_G + g)
        lw = -jnp.exp(-0.5) * _sigmoid(w0_ref[g] + loras[ri][:, g * LANE:(g + 1) * LANE])
        a = _sigmoid(a0_ref[g] + loras[ri][:, RW_WIDTH + g * LANE:RW_WIDTH + (g + 1) * LANE])
        kx = k * kk_ref[g]
        kk = kx * lax.rsqrt(seg_sum(kx * kx) + L2_EPS)
        return r, v, lw, kk, k * (1.0 + (a - 1.0) * ka_ref[g]), kk * a

    r_s, v_s, logw, kk_s, kmod, bb = zip(*[mix_inputs(item) for item in items])

    gam = []
    for n in range(len(row_ids)):
        wide = _cumsum_rows(tri, jnp.concatenate(logw[n * RW_G:(n + 1) * RW_G], axis=1))
        gam += [wide[:, g * LANE:(g + 1) * LANE] for g in range(RW_G)]

    def decay_scaled(x, lw, r, kk, k, b_):
        e_in = jnp.exp(x)
        e_inv = jnp.exp(-x)
        return (e_in, jnp.exp(x[CH - 1:CH, :] - x), r * e_in, kk * jnp.exp(x - lw),
                b_ * e_inv, k * e_inv)

    e_in, e_rem, rt, kt, b_inv, k_inv = zip(*_each(decay_scaled, gam, logw, r_s, kk_s, kmod, bb))

    a_all = _each(
        lambda kt_, rt_, b_, k_: _mm_nt(jnp.concatenate([kt_, rt_], axis=0),
                                        jnp.concatenate([bd(b_), bd(k_)], axis=0)),
        kt, rt, b_inv, k_inv)
    a_kb = [jnp.where(strict, a[:CH, :LANE], 0.0) for a in a_all]
    a_kk = [jnp.where(strict, a[:CH, LANE:], 0.0) for a in a_all]
    a_rb = [jnp.where(incl, a[CH:, :LANE], 0.0) for a in a_all]
    a_rk = [jnp.where(incl, a[CH:, LANE:], 0.0) for a in a_all]
    t_inv = _tri_inverse_each(a_kb)

    av = _each(lambda kk_, rk_, v: _mm(jnp.concatenate([kk_, rk_], axis=0), bd(v)),
               a_kk, a_rk, v_s)
    tx = _each(lambda t, kt_, av_: _mm(t, jnp.concatenate([bd(kt_), bd(av_[:CH])], axis=1)),
               t_inv, kt, av)
    w_nat = [x[:, :LANE] for x in tx]
    u0 = [x[:, LANE:] for x in tx]
    y0 = [x[CH:] for x in av]

    s_old = [s_ref[ri, g] for ri, g in items]
    wh = _each(lambda w, r, s: _mm_nt(jnp.concatenate([w, r], axis=0), s), w_nat, rt, s_old)
    u = _each(lambda wh_, u0_: wh_[:CH] + u0_, wh, u0)
    arbu = _each(lambda a, u_: _mm(a, bd(u_)), a_rb, u)
    y = _each(lambda wh_, y0_, x: wh_[CH:] + y0_ - x, wh, y0, arbu)
    s_add = _each(
        lambda v, u_, k, b, e: _mm_tn(jnp.concatenate([v, u_], axis=0),
                                      jnp.concatenate([k * e, -(b * e)], axis=0)),
        v_s, u, kmod, bb, e_rem)
    for (ri, g), s, e, add in zip(items, s_old, e_in, s_add):
        s_ref[ri, g] = s * e[CH - 1:CH, :] + jnp.where(same_head, add, 0.0)

    mean = _each(lambda y_: seg_sum(y_) * (1.0 / RW_HEAD), y)
    dev = _each(lambda y_, m_: y_ - m_, y, mean)
    var = _each(lambda d: seg_sum(d * d) * (1.0 / RW_HEAD), dev)
    bonus = _each(lambda r, k, item: seg_sum(r * k * rk_ref[item[1]]), r_s, kmod, items)

    def finish(d, var_, bonus_, v, item):
        ri, g = item
        yn = d * lax.rsqrt(var_ + RW_GN_EPS) * gnw_ref[g] + gnb_ref[g] + bonus_ * v
        gate = p_ref[3 * RW_G + 1 + g, ri]
        y_ref[g, ri] = (yn * _silu(gate)).astype(y_ref.dtype)

    _each(finish, dev, var, bonus, v_s, items)


def _rwkv(p, prm, carry_in, state_in, rows):
    _, batch, seq, _ = p.shape
    n_chunks = seq // CH
    full = lambda shape: pl.BlockSpec(shape, lambda b, c: (0,) * len(shape))
    n_carry = 3 * RW_G + 1
    return pl.pallas_call(
        functools.partial(_rwkv_kernel, n_chunks=n_chunks, rows=rows),
        grid=(batch // rows, n_chunks),
        in_specs=[
            pl.BlockSpec((RW_PGROUPS, rows, CH, LANE), lambda b, c: (0, b, c, 0)),
            full((n_carry, 1, LANE)),
            full((RW_G, 1, LANE)), full((RW_G, 1, LANE)), full((RW_G, 1, LANE)),
            full((RW_G, 1, LANE)), full((RW_G, 1, LANE)), full((RW_G, 1, LANE)),
            full((RW_G, 1, LANE)),
            full((2 * RW_LORA, 2 * RW_WIDTH)),
            full((n_carry, SUBLANE, LANE)),
            full((RW_G, LANE, LANE)),
        ],
        out_specs=[
            pl.BlockSpec((RW_G, rows, CH, LANE), lambda b, c: (0, b, c, 0)),
            full((n_carry, SUBLANE, LANE)),
            full((RW_G, LANE, LANE)),
        ],
        out_shape=[
            jax.ShapeDtypeStruct((RW_G, batch, seq, LANE), BF16),
            jax.ShapeDtypeStruct((n_carry, SUBLANE, LANE), F32),
            jax.ShapeDtypeStruct((RW_G, LANE, LANE), F32),
        ],
        scratch_shapes=[
            pltpu.VMEM((rows, RW_G, LANE, LANE), F32),
            pltpu.VMEM((rows, n_carry, SUBLANE + CH, LANE), F32),
        ],
        compiler_params=pltpu.CompilerParams(
            dimension_semantics=("arbitrary", "arbitrary"), vmem_limit_bytes=VMEM_LIMIT),
        name="rwkv7_mix",
    )(p, prm["mu"], prm["w0"], prm["a0"], prm["k_k"], prm["k_a"], prm["r_k"], prm["gn_w"],
      prm["gn_b"], prm["w2a2"], carry_in, state_in)


def _gdn_kernel(p_ref, cw_ref, alog_ref, dtb_ref, nw_ref, cin_ref, s0_ref,
                y_ref, cout_ref, sout_ref, s_ref, c_ref, *, n_chunks, rows):
    c = pl.program_id(1)

    @pl.when(c == 0)
    def _():
        for ri in range(rows):
            s_ref[ri] = s0_ref[...]
            c_ref[ri, :, 0:SUBLANE, :] = cin_ref[...]

    _gdn_rows(p_ref, cw_ref, alog_ref, dtb_ref, nw_ref, y_ref, s_ref, c_ref,
              row_ids=tuple(range(rows)))

    for ri in range(rows):
        c_ref[ri, :, 0:SUBLANE, :] = p_ref[0:3 * DN_HEADS, ri, CH - SUBLANE:CH, :]

    @pl.when(c == n_chunks - 1)
    def _():
        cout_ref[...] = c_ref[rows - 1, :, 0:SUBLANE, :]
        sout_ref[...] = s_ref[rows - 1]


def _gdn_rows(p_ref, cw_ref, alog_ref, dtb_ref, nw_ref, y_ref, s_ref, c_ref, *, row_ids):
    t_i, s_i, lo = _pair_time_index()
    strict = t_i > s_i
    incl = t_i >= s_i
    lo_row = lo[:1]
    tri = _tri_ones()
    zero = jnp.zeros((CH, LANE), F32)
    heads = [(ri, h) for ri in row_ids for h in range(DN_HEADS)]
    pairs = [(n, ri, hp) for n, ri in enumerate(row_ids) for hp in range(DN_HEADS // 2)]
    first = lambda xs: xs[0::2]
    second = lambda xs: xs[1::2]

    def conv_silu(group, ri):
        x = p_ref[group, ri]
        c_ref[ri, group, SUBLANE:, :] = x
        acc = x * cw_ref[CONV_W - 1, group]
        for back in range(1, CONV_W):
            acc = acc + (c_ref[ri, group, SUBLANE - back:SUBLANE - back + CH, :]
                         * cw_ref[CONV_W - 1 - back, group])
        return _silu(acc)

    def l2n(x, scale=1.0):
        return x * (lax.rsqrt(jnp.sum(x * x, axis=-1, keepdims=True) + L2_EPS) * scale)

    beta_all, gc_all, gc_t = {}, {}, {}
    for ri in row_ids:
        ba = p_ref[4 * DN_HEADS, ri]
        beta_all[ri] = _sigmoid(ba)
        z = ba + dtb_ref[...]
        softplus = jnp.maximum(z, 0.0) + jnp.log1p(jnp.exp(-jnp.abs(z)))
        gc_all[ri] = _cumsum_rows(tri, -jnp.exp(alog_ref[...]) * softplus)
        gc_t[ri] = jnp.concatenate([gc_all[ri], gc_all[ri]], axis=0).T

    q = [l2n(conv_silu(h, ri), DN_HEAD ** -0.5) for ri, h in heads]
    k = [l2n(conv_silu(DN_HEADS + h, ri)) for ri, h in heads]
    v = [conv_silu(2 * DN_HEADS + h, ri) for ri, h in heads]
    beta = [beta_all[ri][:, h:h + 1] for ri, h in heads]
    gc = [gc_all[ri][:, DN_HEADS + h:DN_HEADS + h + 1] for ri, h in heads]
    kb = _each(lambda x, b: x * b, k, beta)
    vb = _each(lambda x, b: x * b, v, beta)
    e_gc = _each(jnp.exp, gc)
    qe = _each(lambda x, e: x * e, q, e_gc)
    g_last = [x[CH - 1:CH, :] for x in gc]

    def decay_mask(item):
        n, ri, hp = item
        col = jnp.where(lo, gc[n * DN_HEADS + 2 * hp], gc[n * DN_HEADS + 2 * hp + 1])
        r1 = DN_HEADS + 2 * hp
        rowv = jnp.where(lo_row, gc_t[ri][r1:r1 + 1, :], gc_t[ri][r1 + 1:r1 + 2, :])
        return jnp.where(incl, jnp.exp(col - rowv), 0.0)

    decay = _each(decay_mask, pairs)

    def diag2(a, b):
        return jnp.concatenate([jnp.concatenate([a, zero], axis=1),
                                jnp.concatenate([zero, b], axis=1)], axis=0)

    scores = _each(
        lambda kb1, kb2, q1, q2, k1, k2: _mm_nt(
            jnp.concatenate([jnp.concatenate([kb1, kb2], axis=1),
                             jnp.concatenate([q1, q2], axis=1)], axis=0), diag2(k1, k2)),
        first(kb), second(kb), first(q), second(q), first(k), second(k))
    m_mat = _each(lambda s, d: jnp.where(strict, s[:CH] * d, 0.0), scores, decay)
    attn = _each(lambda s, d: jnp.where(incl, s[CH:] * d, 0.0), scores, decay)
    t_inv = _tri_inverse_each(m_mat)

    kbe = _each(lambda x, e: x * e, kb, e_gc)
    uw = _each(
        lambda t, vb1, vb2, kbe1, kbe2: _mm(t, jnp.concatenate([
            jnp.concatenate([vb1, kbe1, zero, zero], axis=1),
            jnp.concatenate([zero, zero, vb2, kbe2], axis=1)], axis=0)),
        t_inv, first(vb), second(vb), first(kbe), second(kbe))

    s_old = [s_ref[ri, h] for ri, h in heads]
    u_h = [uw[i // 2][:, (i % 2) * 2 * LANE:(i % 2) * 2 * LANE + LANE] for i in range(len(heads))]
    w_h = [uw[i // 2][:, (i % 2) * 2 * LANE + LANE:(i % 2 + 1) * 2 * LANE]
           for i in range(len(heads))]
    ws = _each(lambda w, qe_, s: _mm(jnp.concatenate([w, qe_], axis=0), s), w_h, qe, s_old)
    v_new = _each(lambda u_, ws_: u_ - ws_[:CH], u_h, ws)
    o2 = _each(lambda a, vn1, vn2: _mm(a, diag2(vn1, vn2)),
               attn, first(v_new), second(v_new))

    k_dec = _each(lambda x, gl, g: x * jnp.exp(gl - g), k, g_last, gc)
    s_add = _each(_mm_tn, k_dec, v_new)
    for i, (ri, h) in enumerate(heads):
        s_ref[ri, h] = s_old[i] * jnp.exp(g_last[i]) + s_add[i]
        o = ws[i][CH:] + o2[i // 2][:, (i % 2) * LANE:(i % 2 + 1) * LANE]
        o = o * lax.rsqrt(jnp.mean(o * o, axis=-1, keepdims=True) + NORM_EPS) * nw_ref[...]
        zg = p_ref[3 * DN_HEADS + h, ri]
        y_ref[h, ri] = (o * _silu(zg)).astype(y_ref.dtype)


def _gdn(p, prm, carry_in, state_in, rows):
    _, batch, seq, _ = p.shape
    n_chunks = seq // CH
    full = lambda shape: pl.BlockSpec(shape, lambda b, c: (0,) * len(shape))
    n_carry = 3 * DN_HEADS
    return pl.pallas_call(
        functools.partial(_gdn_kernel, n_chunks=n_chunks, rows=rows),
        grid=(batch // rows, n_chunks),
        in_specs=[
            pl.BlockSpec((DN_PGROUPS, rows, CH, LANE), lambda b, c: (1, b, c, 0)),
            full((CONV_W, n_carry, 1, LANE)),
            full((1, LANE)), full((1, LANE)), full((1, LANE)),
            full((n_carry, SUBLANE, LANE)),
            full((DN_HEADS, DN_HEAD, DN_HEAD)),
        ],
        out_specs=[
            pl.BlockSpec((DN_HEADS, rows, CH, LANE), lambda b, c: (0, b, c, 0)),
            full((n_carry, SUBLANE, LANE)),
            full((DN_HEADS, DN_HEAD, DN_HEAD)),
        ],
        out_shape=[
            jax.ShapeDtypeStruct((DN_HEADS, batch, seq, LANE), BF16),
            jax.ShapeDtypeStruct((n_carry, SUBLANE, LANE), F32),
            jax.ShapeDtypeStruct((DN_HEADS, DN_HEAD, DN_HEAD), F32),
        ],
        scratch_shapes=[
            pltpu.VMEM((rows, DN_HEADS, DN_HEAD, DN_HEAD), F32),
            pltpu.VMEM((rows, n_carry, SUBLANE + CH, LANE), F32),
        ],
        compiler_params=pltpu.CompilerParams(
            dimension_semantics=("arbitrary", "arbitrary"), vmem_limit_bytes=VMEM_LIMIT),
        name="gated_delta_mix",
    )(p, prm["conv_w"], prm["a_log"], prm["dt_bias"], prm["norm_w"], carry_in, state_in)


def _outproj_kernel(ya_ref, yb_ref, x_ref, w_ref, fnw_ref, o_ref):
    y = jnp.concatenate([ya_ref[g] for g in range(RW_G)] + [yb_ref[h] for h in range(DN_HEADS)],
                        axis=1)
    hid = x_ref[...] + jnp.dot(y, w_ref[...], preferred_element_type=F32)
    ms = jnp.mean(hid * hid, axis=-1, keepdims=True)
    o_ref[...] = hid * lax.rsqrt(ms + NORM_EPS) * fnw_ref[...]


def _outproj(ya, yb, x2d, w_out, fnw, tm):
    m, d = x2d.shape
    return pl.pallas_call(
        _outproj_kernel,
        grid=(m // tm,),
        in_specs=[
            pl.BlockSpec((RW_G, tm, LANE), lambda i: (0, i, 0)),
            pl.BlockSpec((DN_HEADS, tm, LANE), lambda i: (0, i, 0)),
            pl.BlockSpec((tm, d), lambda i: (i, 0)),
            pl.BlockSpec((RW_WIDTH + DN_WIDTH, d), lambda i: (0, 0)),
            pl.BlockSpec((1, d), lambda i: (0, 0)),
        ],
        out_specs=pl.BlockSpec((tm, d), lambda i: (i, 0)),
        out_shape=jax.ShapeDtypeStruct((m, d), F32),
        compiler_params=pltpu.CompilerParams(
            dimension_semantics=("arbitrary",), vmem_limit_bytes=VMEM_LIMIT),
        name="outproj",
    )(ya, yb, x2d, w_out, fnw)


def _row_tile(m, cap):
    t = cap
    while m % t:
        t //= 2
    return t


def kernel(x, meta_tokens, norm_w, w_in, rw_shift_mu, rw_w0, rw_w2, rw_a0, rw_a2, rw_k_k, rw_k_a,
           rw_r_k, rw_gn_w, rw_gn_b, dn_conv_w, dn_A_log, dn_dt_bias, dn_norm_w, w_out,
           final_norm_w):
    batch, seq, d = x.shape
    assert seq % CH == 0 and norm_w.shape[0] == 1
    m = batch * seq

    tn = 768
    wi = w_in[0].T.astype(BF16)
    c_b = 3 * RW_WIDTH + 2 * RW_LORA + RW_WIDTH + 3 * DN_WIDTH
    c_z = c_b + 2 * DN_HEADS
    c_tail = (P_GROUPS * LANE // tn - 2) * tn
    assert c_tail <= c_b and c_tail % LANE == 0
    w_tail = jnp.concatenate([
        wi[c_tail:c_b], wi[c_z:], wi[c_b:c_z],
        jnp.zeros((LANE - 2 * DN_HEADS, d), wi.dtype)], axis=0)
    grp = lambda t: t.reshape(-1, 1, LANE)
    zeros_l = jnp.zeros((RW_LORA, RW_WIDTH), F32)
    rw_prm = {
        "mu": grp(rw_shift_mu[0]),
        "w0": grp(rw_w0[0]), "a0": grp(rw_a0[0]), "k_k": grp(rw_k_k[0]), "k_a": grp(rw_k_a[0]),
        "r_k": grp(rw_r_k[0]), "gn_w": grp(rw_gn_w[0]), "gn_b": grp(rw_gn_b[0]),
        "w2a2": jnp.concatenate([jnp.concatenate([rw_w2[0], zeros_l], axis=1),
                                 jnp.concatenate([zeros_l, rw_a2[0]], axis=1)], axis=0).astype(BF16),
    }
    lane_vec = lambda t: jnp.zeros((1, LANE), F32).at[0, DN_HEADS:2 * DN_HEADS].set(t)
    dn_prm = {
        "conv_w": dn_conv_w[0].reshape(CONV_W, 3 * DN_HEADS, 1, LANE),
        "a_log": lane_vec(dn_A_log[0]), "dt_bias": lane_vec(dn_dt_bias[0]),
        "norm_w": dn_norm_w[0].reshape(1, LANE),
    }
    nw = norm_w[0].reshape(1, d)

    meta_rows = jnp.concatenate([jnp.zeros((CH - N_META, d), x.dtype), meta_tokens.astype(x.dtype)],
                                axis=0)
    p_meta = _inproj(meta_rows, nw, wi, w_tail, CH, tn).reshape(P_GROUPS, 1, CH, LANE)
    zc = lambda n: jnp.zeros((n, SUBLANE, LANE), F32)
    _, rw_carry, rw_state = _rwkv(p_meta, rw_prm, zc(3 * RW_G + 1),
                                  jnp.zeros((RW_G, LANE, LANE), F32), 1)
    _, dn_carry, dn_state = _gdn(p_meta, dn_prm, zc(3 * DN_HEADS),
                                 jnp.zeros((DN_HEADS, DN_HEAD, DN_HEAD), F32), 1)

    x2d = x.reshape(m, d)
    p = _inproj(x2d, nw, wi, w_tail, _row_tile(m, 1024), tn).reshape(P_GROUPS, batch, seq, LANE)
    ya, _, _ = _rwkv(p, rw_prm, rw_carry, rw_state, 4 if batch % 4 == 0 else 1)
    yb, _, _ = _gdn(p, dn_prm, dn_carry, dn_state, 4 if batch % 4 == 0 else 1)
    out = _outproj(ya.reshape(RW_G, m, LANE), yb.reshape(DN_HEADS, m, LANE), x2d,
                   w_out[0].astype(BF16), final_norm_w.reshape(1, d), _row_tile(m, 512))
    return out.reshape(batch, seq, d)
```

```python
import functools

import jax
import jax.numpy as jnp
from jax import lax
from jax.experimental import pallas as pl
from jax.experimental.pallas import tpu as pltpu

F32 = jnp.float32
BF16 = jnp.bfloat16

LANE = 128
SUBLANE = 8
CH = 64
N_META = 16
RW_WIDTH = 1024
RW_HEAD = 64
RW_LORA = 64
DN_WIDTH = 1024
DN_HEAD = 128
DN_HEADS = DN_WIDTH // DN_HEAD
CONV_W = 4
NORM_EPS = 1e-6
L2_EPS = 1e-6
RW_GN_EPS = 64e-5
RW_G = RW_WIDTH // LANE
RW_PGROUPS = 4 * RW_G + 1
DN_PGROUPS = 4 * DN_HEADS + 1
P_GROUPS = RW_PGROUPS + DN_PGROUPS
VMEM_LIMIT = 56 * 1024 * 1024


def _mm(a, b):
    return jnp.dot(a.astype(BF16), b.astype(BF16), preferred_element_type=F32)


def _mm_nt(a, b):
    return lax.dot_general(a.astype(BF16), b.astype(BF16), (((1,), (1,)), ((), ())),
                           preferred_element_type=F32)


def _mm_tn(a, b):
    return lax.dot_general(a.astype(BF16), b.astype(BF16), (((0,), (0,)), ((), ())),
                           preferred_element_type=F32)


def _each(fn, *lists):
    return [fn(*args) for args in zip(*lists)]


def _cumsum_rows(tri, x):
    hi = x.astype(BF16)
    lo = (x - hi.astype(F32)).astype(BF16)
    dot = lambda h: jnp.dot(tri, h, preferred_element_type=F32)
    return dot(hi) + dot(lo)


def _sigmoid(z):
    return 0.5 * jnp.tanh(0.5 * z) + 0.5


def _silu(z):
    h = 0.5 * z
    return h + h * jnp.tanh(h)


def _pair_time_index():
    t = lax.broadcasted_iota(jnp.int32, (CH, 2 * CH), 0)
    lane = lax.broadcasted_iota(jnp.int32, (CH, 2 * CH), 1)
    return t, lane & (CH - 1), lane < CH


def _block_diag(x, first):
    xb = x.astype(BF16)
    zero = jnp.zeros_like(xb)
    return jnp.concatenate([jnp.where(first, xb, zero), jnp.where(first, zero, xb)], axis=0)


def _tri_inverse_each(mats):
    i, j, first = _pair_time_index()
    eye = (i == j).astype(BF16)
    zero = jnp.zeros_like(eye)

    def off(m):
        return ((i & -(2 * m)) == (j & -(2 * m))) & ((i & m) != 0) & ((j & m) == 0)

    neg = [(-a).astype(BF16) for a in mats]
    ts = [jnp.where(off(1), na, eye) for na in neg]
    m = 2
    while m < CH:
        mask = off(m)
        prods = _each(lambda na, t: _mm(jnp.where(mask, na, zero), _block_diag(t, first)), neg, ts)
        corr = _each(lambda t, p: _mm(t, _block_diag(p, first)), ts, prods)
        ts = _each(lambda t, c: jnp.where(mask, c.astype(BF16), t), ts, corr)
        m *= 2
    return ts


def _same_head_mask():
    i2 = lax.broadcasted_iota(jnp.int32, (2 * CH, 2 * CH), 0)
    j2 = lax.broadcasted_iota(jnp.int32, (2 * CH, 2 * CH), 1)
    return (i2 >= CH) == (j2 >= CH)


def _tri_ones():
    ti = lax.broadcasted_iota(jnp.int32, (CH, CH), 0)
    tj = lax.broadcasted_iota(jnp.int32, (CH, CH), 1)
    return (ti >= tj).astype(BF16)


def _inproj_kernel(x_ref, nw_ref, wa_ref, wb_ref, o_ref, u_ref, *, n_sub, n_main):
    j = pl.program_id(1)

    @pl.when(j == 0)
    def _():
        x = x_ref[...]
        ms = jnp.mean(x * x, axis=-1, keepdims=True)
        u_ref[...] = (x * lax.rsqrt(ms + NORM_EPS) * nw_ref[...]).astype(BF16)

    def project(w_t):
        acc = lax.dot_general(u_ref[...], w_t, (((1,), (1,)), ((), ())),
                              preferred_element_type=F32)
        for s in range(n_sub):
            o_ref[s] = acc[:, s * LANE:(s + 1) * LANE]

    @pl.when(j < n_main)
    def _():
        project(wa_ref[...])

    @pl.when(j >= n_main)
    def _():
        project(wb_ref[...])


def _inproj(x2d, norm_w, w_main, w_tail, tm, tn):
    m, d = x2d.shape
    n_tail = w_tail.shape[0] // tn
    n_main = P_GROUPS * LANE // tn - n_tail
    n_sub = tn // LANE
    return pl.pallas_call(
        functools.partial(_inproj_kernel, n_sub=n_sub, n_main=n_main),
        grid=(m // tm, n_main + n_tail),
        in_specs=[
            pl.BlockSpec((tm, d), lambda i, j: (i, 0)),
            pl.BlockSpec((1, d), lambda i, j: (0, 0)),
            pl.BlockSpec((tn, d), lambda i, j: (jnp.minimum(j, n_main - 1), 0)),
            pl.BlockSpec((tn, d), lambda i, j: (jnp.maximum(j - n_main, 0), 0)),
        ],
        out_specs=pl.BlockSpec((n_sub, tm, LANE), lambda i, j: (j, i, 0)),
        out_shape=jax.ShapeDtypeStruct((P_GROUPS, m, LANE), F32),
        scratch_shapes=[pltpu.VMEM((tm, d), BF16)],
        compiler_params=pltpu.CompilerParams(
            dimension_semantics=("arbitrary", "arbitrary"), vmem_limit_bytes=VMEM_LIMIT),
        name="inproj",
    )(x2d, norm_w, w_main, w_tail)


def _rwkv_rows(p_ref, mu_ref, w0_ref, a0_ref, kk_ref, ka_ref, rk_ref, gnw_ref, gnb_ref,
               w2a2_ref, y_ref, s_ref, c_ref, *, row_ids):
    t_i, s_i, lo = _pair_time_index()
    strict = t_i > s_i
    incl = t_i >= s_i
    same_head = _same_head_mask()
    tri = _tri_ones()
    items = [(ri, g) for ri in row_ids for g in range(RW_G)]
    bd = lambda x: _block_diag(x, lo)

    def seg_sum(x):
        s_lo = jnp.sum(jnp.where(lo, x, 0.0), axis=-1, keepdims=True)
        s_hi = jnp.sum(jnp.where(lo, 0.0, x), axis=-1, keepdims=True)
        return jnp.where(lo, s_lo, s_hi)

    def shifted(group, ri, mu_idx, tail_idx):
        x = p_ref[group, ri]
        c_ref[ri, tail_idx, SUBLANE:, :] = x
        prev = c_ref[ri, tail_idx, SUBLANE - 1:SUBLANE - 1 + CH, :]
        return x + (prev - x) * mu_ref[mu_idx]

    loras = {}
    for ri in row_ids:
        lg = shifted(3 * RW_G, ri, 3 * RW_G, 3 * RW_G)
        xl = jnp.where(lo, jnp.tanh(lg), lg)
        loras[ri] = jnp.dot(xl.astype(BF16), w2a2_ref[...], preferred_element_type=F32)

    def mix_inputs(item):
        ri, g = item
        r = shifted(g, ri, g, g)
        k = shifted(RW_G + g, ri, RW_G + g, RW_G + g)
        v = shifted(2 * RW_G + g, ri, 2 * RW_G + g, 2 * RW_G + g)
        lw = -jnp.exp(-0.5) * _sigmoid(w0_ref[g] + loras[ri][:, g * LANE:(g + 1) * LANE])
        a = _sigmoid(a0_ref[g] + loras[ri][:, RW_WIDTH + g * LANE:RW_WIDTH + (g + 1) * LANE])
        kx = k * kk_ref[g]
        kk = kx * lax.rsqrt(seg_sum(kx * kx) + L2_EPS)
        return r, v, lw, kk, k * (1.0 + (a - 1.0) * ka_ref[g]), kk * a

    r_s, v_s, logw, kk_s, kmod, bb = zip(*[mix_inputs(item) for item in items])

    gam = []
    for n in range(len(row_ids)):
        wide = _cumsum_rows(tri, jnp.concatenate(logw[n * RW_G:(n + 1) * RW_G], axis=1))
        gam += [wide[:, g * LANE:(g + 1) * LANE] for g in range(RW_G)]

    def decay_scaled(x, lw, r, kk, k, b_):
        e_in = jnp.exp(x)
        e_inv = jnp.exp(-x)
        return (e_in, jnp.exp(x[CH - 1:CH, :] - x), r * e_in, kk * jnp.exp(x - lw),
                b_ * e_inv, k * e_inv)

    e_in, e_rem, rt, kt, b_inv, k_inv = zip(*_each(decay_scaled, gam, logw, r_s, kk_s, kmod, bb))

    a_all = _each(
        lambda kt_, rt_, b_, k_: _mm_nt(jnp.concatenate([kt_, rt_], axis=0),
                                        jnp.concatenate([bd(b_), bd(k_)], axis=0)),
        kt, rt, b_inv, k_inv)
    a_kb = [jnp.where(strict, a[:CH, :LANE], 0.0) for a in a_all]
    a_kk = [jnp.where(strict, a[:CH, LANE:], 0.0) for a in a_all]
    a_rb = [jnp.where(incl, a[CH:, :LANE], 0.0) for a in a_all]
    a_rk = [jnp.where(incl, a[CH:, LANE:], 0.0) for a in a_all]
    t_inv = yield a_kb

    av = _each(lambda kk_, rk_, v: _mm(jnp.concatenate([kk_, rk_], axis=0), bd(v)),
               a_kk, a_rk, v_s)
    yield
    tx = _each(lambda t, kt_, av_: _mm(t, jnp.concatenate([bd(kt_), bd(av_[:CH])], axis=1)),
               t_inv, kt, av)
    w_nat = [x[:, :LANE] for x in tx]
    u0 = [x[:, LANE:] for x in tx]
    y0 = [x[CH:] for x in av]

    yield
    s_old = [s_ref[ri, g] for ri, g in items]
    wh = _each(lambda w, r, s: _mm_nt(jnp.concatenate([w, r], axis=0), s), w_nat, rt, s_old)
    yield
    u = _each(lambda wh_, u0_: wh_[:CH] + u0_, wh, u0)
    arbu = _each(lambda a, u_: _mm(a, bd(u_)), a_rb, u)
    yield
    y = _each(lambda wh_, y0_, x: wh_[CH:] + y0_ - x, wh, y0, arbu)
    s_add = _each(
        lambda v, u_, k, b, e: _mm_tn(jnp.concatenate([v, u_], axis=0),
                                      jnp.concatenate([k * e, -(b * e)], axis=0)),
        v_s, u, kmod, bb, e_rem)
    for (ri, g), s, e, add in zip(items, s_old, e_in, s_add):
        s_ref[ri, g] = s * e[CH - 1:CH, :] + jnp.where(same_head, add, 0.0)

    yield
    mean = _each(lambda y_: seg_sum(y_) * (1.0 / RW_HEAD), y)
    dev = _each(lambda y_, m_: y_ - m_, y, mean)
    var = _each(lambda d: seg_sum(d * d) * (1.0 / RW_HEAD), dev)
    bonus = _each(lambda r, k, item: seg_sum(r * k * rk_ref[item[1]]), r_s, kmod, items)

    def finish(d, var_, bonus_, v, item):
        ri, g = item
        yn = d * lax.rsqrt(var_ + RW_GN_EPS) * gnw_ref[g] + gnb_ref[g] + bonus_ * v
        gate = p_ref[3 * RW_G + 1 + g, ri]
        y_ref[g, ri] = (yn * _silu(gate)).astype(y_ref.dtype)

    _each(finish, dev, var, bonus, v_s, items)


def _gdn_rows(p_ref, cw_ref, alog_ref, dtb_ref, nw_ref, y_ref, s_ref, c_ref, *, row_ids):
    t_i, s_i, lo = _pair_time_index()
    strict = t_i > s_i
    incl = t_i >= s_i
    lo_row = lo[:1]
    tri = _tri_ones()
    zero = jnp.zeros((CH, LANE), F32)
    heads = [(ri, h) for ri in row_ids for h in range(DN_HEADS)]
    pairs = [(n, ri, hp) for n, ri in enumerate(row_ids) for hp in range(DN_HEADS // 2)]
    first = lambda xs: xs[0::2]
    second = lambda xs: xs[1::2]

    def conv_silu(group, ri):
        x = p_ref[group, ri]
        c_ref[ri, group, SUBLANE:, :] = x
        acc = x * cw_ref[CONV_W - 1, group]
        for back in range(1, CONV_W):
            acc = acc + (c_ref[ri, group, SUBLANE - back:SUBLANE - back + CH, :]
                         * cw_ref[CONV_W - 1 - back, group])
        return _silu(acc)

    def l2n(x, scale=1.0):
        return x * (lax.rsqrt(jnp.sum(x * x, axis=-1, keepdims=True) + L2_EPS) * scale)

    beta_all, gc_all, gc_t = {}, {}, {}
    for ri in row_ids:
        ba = p_ref[4 * DN_HEADS, ri]
        beta_all[ri] = _sigmoid(ba)
        z = ba + dtb_ref[...]
        softplus = jnp.maximum(z, 0.0) + jnp.log1p(jnp.exp(-jnp.abs(z)))
        gc_all[ri] = _cumsum_rows(tri, -jnp.exp(alog_ref[...]) * softplus)
        gc_t[ri] = jnp.concatenate([gc_all[ri], gc_all[ri]], axis=0).T

    q = [l2n(conv_silu(h, ri), DN_HEAD ** -0.5) for ri, h in heads]
    k = [l2n(conv_silu(DN_HEADS + h, ri)) for ri, h in heads]
    v = [conv_silu(2 * DN_HEADS + h, ri) for ri, h in heads]
    beta = [beta_all[ri][:, h:h + 1] for ri, h in heads]
    gc = [gc_all[ri][:, DN_HEADS + h:DN_HEADS + h + 1] for ri, h in heads]
    kb = _each(lambda x, b: x * b, k, beta)
    vb = _each(lambda x, b: x * b, v, beta)
    e_gc = _each(jnp.exp, gc)
    qe = _each(lambda x, e: x * e, q, e_gc)
    g_last = [x[CH - 1:CH, :] for x in gc]

    def decay_mask(item):
        n, ri, hp = item
        col = jnp.where(lo, gc[n * DN_HEADS + 2 * hp], gc[n * DN_HEADS + 2 * hp + 1])
        r1 = DN_HEADS + 2 * hp
        rowv = jnp.where(lo_row, gc_t[ri][r1:r1 + 1, :], gc_t[ri][r1 + 1:r1 + 2, :])
        return jnp.where(incl, jnp.exp(col - rowv), 0.0)

    decay = _each(decay_mask, pairs)

    def diag2(a, b):
        return jnp.concatenate([jnp.concatenate([a, zero], axis=1),
                                jnp.concatenate([zero, b], axis=1)], axis=0)

    scores = _each(
        lambda kb1, kb2, q1, q2, k1, k2: _mm_nt(
            jnp.concatenate([jnp.concatenate([kb1, kb2], axis=1),
                             jnp.concatenate([q1, q2], axis=1)], axis=0), diag2(k1, k2)),
        first(kb), second(kb), first(q), second(q), first(k), second(k))
    m_mat = _each(lambda s, d: jnp.where(strict, s[:CH] * d, 0.0), scores, decay)
    attn = _each(lambda s, d: jnp.where(incl, s[CH:] * d, 0.0), scores, decay)
    t_inv = yield m_mat

    kbe = _each(lambda x, e: x * e, kb, e_gc)
    uw = _each(
        lambda t, vb1, vb2, kbe1, kbe2: _mm(t, jnp.concatenate([
            jnp.concatenate([vb1, kbe1, zero, zero], axis=1),
            jnp.concatenate([zero, zero, vb2, kbe2], axis=1)], axis=0)),
        t_inv, first(vb), second(vb), first(kbe), second(kbe))

    yield
    s_old = [s_ref[ri, h] for ri, h in heads]
    u_h = [uw[i // 2][:, (i % 2) * 2 * LANE:(i % 2) * 2 * LANE + LANE] for i in range(len(heads))]
    w_h = [uw[i // 2][:, (i % 2) * 2 * LANE + LANE:(i % 2 + 1) * 2 * LANE]
           for i in range(len(heads))]
    ws = _each(lambda w, qe_, s: _mm(jnp.concatenate([w, qe_], axis=0), s), w_h, qe, s_old)
    yield
    v_new = _each(lambda u_, ws_: u_ - ws_[:CH], u_h, ws)
    o2 = _each(lambda a, vn1, vn2: _mm(a, diag2(vn1, vn2)),
               attn, first(v_new), second(v_new))

    yield
    k_dec = _each(lambda x, gl, g: x * jnp.exp(gl - g), k, g_last, gc)
    s_add = _each(_mm_tn, k_dec, v_new)
    yield
    for i, (ri, h) in enumerate(heads):
        s_ref[ri, h] = s_old[i] * jnp.exp(g_last[i]) + s_add[i]
        o = ws[i][CH:] + o2[i // 2][:, (i % 2) * LANE:(i % 2 + 1) * LANE]
        o = o * lax.rsqrt(jnp.mean(o * o, axis=-1, keepdims=True) + NORM_EPS) * nw_ref[...]
        zg = p_ref[3 * DN_HEADS + h, ri]
        y_ref[h, ri] = (o * _silu(zg)).astype(y_ref.dtype)


def _mix_kernel(p_ref, mu_ref, w0_ref, a0_ref, kk_ref, ka_ref, rk_ref, gnw_ref, gnb_ref, w2a2_ref,
                cw_ref, alog_ref, dtb_ref, nw_ref, rw_cin_ref, rw_s0_ref, dn_cin_ref, dn_s0_ref,
                ya_ref, yb_ref, rw_cout_ref, rw_sout_ref, dn_cout_ref, dn_sout_ref,
                rw_s_ref, rw_c_ref, dn_s_ref, dn_c_ref, *, n_chunks, rows):
    c = pl.program_id(1)
    dn_p_ref = p_ref.at[RW_PGROUPS:P_GROUPS]

    @pl.when(c == 0)
    def _():
        for ri in range(rows):
            rw_s_ref[ri] = rw_s0_ref[...]
            rw_c_ref[ri, :, 0:SUBLANE, :] = rw_cin_ref[...]
            dn_s_ref[ri] = dn_s0_ref[...]
            dn_c_ref[ri, :, 0:SUBLANE, :] = dn_cin_ref[...]

    row_ids = tuple(range(rows))
    mixers = [
        _rwkv_rows(p_ref, mu_ref, w0_ref, a0_ref, kk_ref, ka_ref, rk_ref, gnw_ref, gnb_ref,
                   w2a2_ref, ya_ref, rw_s_ref, rw_c_ref, row_ids=row_ids),
        _gdn_rows(dn_p_ref, cw_ref, alog_ref, dtb_ref, nw_ref, yb_ref, dn_s_ref, dn_c_ref,
                  row_ids=row_ids),
    ]
    wanted = [next(mixer) for mixer in mixers]
    inverses = _tri_inverse_each([mat for mats in wanted for mat in mats])
    live = []
    for mixer, mats in zip(mixers, wanted):
        mixer.send(inverses[:len(mats)])
        inverses = inverses[len(mats):]
        live.append(mixer)
    while live:
        for mixer in list(live):
            try:
                next(mixer)
            except StopIteration:
                live.remove(mixer)

    for ri in range(rows):
        rw_c_ref[ri, :, 0:SUBLANE, :] = p_ref[0:3 * RW_G + 1, ri, CH - SUBLANE:CH, :]
        dn_c_ref[ri, :, 0:SUBLANE, :] = dn_p_ref[0:3 * DN_HEADS, ri, CH - SUBLANE:CH, :]

    @pl.when(c == n_chunks - 1)
    def _():
        rw_cout_ref[...] = rw_c_ref[rows - 1, :, 0:SUBLANE, :]
        rw_sout_ref[...] = rw_s_ref[rows - 1]
        dn_cout_ref[...] = dn_c_ref[rows - 1, :, 0:SUBLANE, :]
        dn_sout_ref[...] = dn_s_ref[rows - 1]


def _mix(p, rw_prm, dn_prm, rw_carry, rw_state, dn_carry, dn_state, rows):
    _, batch, seq, _ = p.shape
    n_chunks = seq // CH
    full = lambda shape: pl.BlockSpec(shape, lambda b, c: (0,) * len(shape))
    rw_carry_n = 3 * RW_G + 1
    dn_carry_n = 3 * DN_HEADS
    seq_block = lambda groups: pl.BlockSpec((groups, rows, CH, LANE), lambda b, c: (0, b, c, 0))
    rw_group = full((RW_G, 1, LANE))
    return pl.pallas_call(
        functools.partial(_mix_kernel, n_chunks=n_chunks, rows=rows),
        grid=(batch // rows, n_chunks),
        in_specs=[
            seq_block(P_GROUPS),
            full((rw_carry_n, 1, LANE)),
            rw_group, rw_group, rw_group, rw_group, rw_group, rw_group, rw_group,
            full((2 * RW_LORA, 2 * RW_WIDTH)),
            full((CONV_W, dn_carry_n, 1, LANE)),
            full((1, LANE)), full((1, LANE)), full((1, LANE)),
            full((rw_carry_n, SUBLANE, LANE)), full((RW_G, LANE, LANE)),
            full((dn_carry_n, SUBLANE, LANE)), full((DN_HEADS, DN_HEAD, DN_HEAD)),
        ],
        out_specs=[
            seq_block(RW_G), seq_block(DN_HEADS),
            full((rw_carry_n, SUBLANE, LANE)), full((RW_G, LANE, LANE)),
            full((dn_carry_n, SUBLANE, LANE)), full((DN_HEADS, DN_HEAD, DN_HEAD)),
        ],
        out_shape=[
            jax.ShapeDtypeStruct((RW_G, batch, seq, LANE), BF16),
            jax.ShapeDtypeStruct((DN_HEADS, batch, seq, LANE), BF16),
            jax.ShapeDtypeStruct((rw_carry_n, SUBLANE, LANE), F32),
            jax.ShapeDtypeStruct((RW_G, LANE, LANE), F32),
            jax.ShapeDtypeStruct((dn_carry_n, SUBLANE, LANE), F32),
            jax.ShapeDtypeStruct((DN_HEADS, DN_HEAD, DN_HEAD), F32),
        ],
        scratch_shapes=[
            pltpu.VMEM((rows, RW_G, LANE, LANE), F32),
            pltpu.VMEM((rows, rw_carry_n, SUBLANE + CH, LANE), F32),
            pltpu.VMEM((rows, DN_HEADS, DN_HEAD, DN_HEAD), F32),
            pltpu.VMEM((rows, dn_carry_n, SUBLANE + CH, LANE), F32),
        ],
        compiler_params=pltpu.CompilerParams(
            dimension_semantics=("arbitrary", "arbitrary"), vmem_limit_bytes=VMEM_LIMIT),
        name="mixers",
    )(p, rw_prm["mu"], rw_prm["w0"], rw_prm["a0"], rw_prm["k_k"], rw_prm["k_a"], rw_prm["r_k"],
      rw_prm["gn_w"], rw_prm["gn_b"], rw_prm["w2a2"], dn_prm["conv_w"], dn_prm["a_log"],
      dn_prm["dt_bias"], dn_prm["norm_w"], rw_carry, rw_state, dn_carry, dn_state)


def _outproj_kernel(ya_ref, yb_ref, x_ref, w_ref, fnw_ref, o_ref):
    y = jnp.concatenate([ya_ref[g] for g in range(RW_G)] + [yb_ref[h] for h in range(DN_HEADS)],
                        axis=1)
    hid = x_ref[...] + jnp.dot(y, w_ref[...], preferred_element_type=F32)
    ms = jnp.mean(hid * hid, axis=-1, keepdims=True)
    o_ref[...] = hid * lax.rsqrt(ms + NORM_EPS) * fnw_ref[...]


def _outproj(ya, yb, x2d, w_out, fnw, tm):
    m, d = x2d.shape
    return pl.pallas_call(
        _outproj_kernel,
        grid=(m // tm,),
        in_specs=[
            pl.BlockSpec((RW_G, tm, LANE), lambda i: (0, i, 0)),
            pl.BlockSpec((DN_HEADS, tm, LANE), lambda i: (0, i, 0)),
            pl.BlockSpec((tm, d), lambda i: (i, 0)),
            pl.BlockSpec((RW_WIDTH + DN_WIDTH, d), lambda i: (0, 0)),
            pl.BlockSpec((1, d), lambda i: (0, 0)),
        ],
        out_specs=pl.BlockSpec((tm, d), lambda i: (i, 0)),
        out_shape=jax.ShapeDtypeStruct((m, d), F32),
        compiler_params=pltpu.CompilerParams(
            dimension_semantics=("arbitrary",), vmem_limit_bytes=VMEM_LIMIT),
        name="outproj",
    )(ya, yb, x2d, w_out, fnw)


def _row_tile(m, cap):
    t = cap
    while m % t:
        t //= 2
    return t


def kernel(x, meta_tokens, norm_w, w_in, rw_shift_mu, rw_w0, rw_w2, rw_a0, rw_a2, rw_k_k, rw_k_a,
           rw_r_k, rw_gn_w, rw_gn_b, dn_conv_w, dn_A_log, dn_dt_bias, dn_norm_w, w_out,
           final_norm_w):
    batch, seq, d = x.shape
    assert seq % CH == 0 and norm_w.shape[0] == 1
    m = batch * seq

    tn = 768
    wi = w_in[0].T.astype(BF16)
    c_b = 3 * RW_WIDTH + 2 * RW_LORA + RW_WIDTH + 3 * DN_WIDTH
    c_z = c_b + 2 * DN_HEADS
    c_tail = (P_GROUPS * LANE // tn - 2) * tn
    assert c_tail <= c_b and c_tail % LANE == 0
    w_tail = jnp.concatenate([
        wi[c_tail:c_b], wi[c_z:], wi[c_b:c_z],
        jnp.zeros((LANE - 2 * DN_HEADS, d), wi.dtype)], axis=0)
    grp = lambda t: t.reshape(-1, 1, LANE)
    zeros_l = jnp.zeros((RW_LORA, RW_WIDTH), F32)
    rw_prm = {
        "mu": grp(rw_shift_mu[0]),
        "w0": grp(rw_w0[0]), "a0": grp(rw_a0[0]), "k_k": grp(rw_k_k[0]), "k_a": grp(rw_k_a[0]),
        "r_k": grp(rw_r_k[0]), "gn_w": grp(rw_gn_w[0]), "gn_b": grp(rw_gn_b[0]),
        "w2a2": jnp.concatenate([jnp.concatenate([rw_w2[0], zeros_l], axis=1),
                                 jnp.concatenate([zeros_l, rw_a2[0]], axis=1)], axis=0).astype(BF16),
    }
    lane_vec = lambda t: jnp.zeros((1, LANE), F32).at[0, DN_HEADS:2 * DN_HEADS].set(t)
    dn_prm = {
        "conv_w": dn_conv_w[0].reshape(CONV_W, 3 * DN_HEADS, 1, LANE),
        "a_log": lane_vec(dn_A_log[0]), "dt_bias": lane_vec(dn_dt_bias[0]),
        "norm_w": dn_norm_w[0].reshape(1, LANE),
    }
    nw = norm_w[0].reshape(1, d)

    meta_rows = jnp.concatenate([jnp.zeros((CH - N_META, d), x.dtype), meta_tokens.astype(x.dtype)],
                                axis=0)
    p_meta = _inproj(meta_rows, nw, wi, w_tail, CH, tn).reshape(P_GROUPS, 1, CH, LANE)
    zc = lambda n: jnp.zeros((n, SUBLANE, LANE), F32)
    _, _, rw_carry, rw_state, dn_carry, dn_state = _mix(
        p_meta, rw_prm, dn_prm, zc(3 * RW_G + 1), jnp.zeros((RW_G, LANE, LANE), F32),
        zc(3 * DN_HEADS), jnp.zeros((DN_HEADS, DN_HEAD, DN_HEAD), F32), 1)

    x2d = x.reshape(m, d)
    p = _inproj(x2d, nw, wi, w_tail, _row_tile(m, 1024), tn).reshape(P_GROUPS, batch, seq, LANE)
    ya, yb, _, _, _, _ = _mix(p, rw_prm, dn_prm, rw_carry, rw_state, dn_carry, dn_state,
                              4 if batch % 4 == 0 else 1)
    out = _outproj(ya.reshape(RW_G, m, LANE), yb.reshape(DN_HEADS, m, LANE), x2d,
                   w_out[0].astype(BF16), final_norm_w.reshape(1, d), _row_tile(m, 512))
    return out.reshape(batch, seq, d)
```

```python
import functools

import jax
import jax.numpy as jnp
from jax import lax
from jax.experimental import pallas as pl
from jax.experimental.pallas import tpu as pltpu

F32 = jnp.float32
BF16 = jnp.bfloat16

LANE = 128
SUBLANE = 8
CH = 64
N_META = 16
RW_WIDTH = 1024
RW_HEAD = 64
RW_LORA = 64
DN_WIDTH = 1024
DN_HEAD = 128
DN_HEADS = DN_WIDTH // DN_HEAD
CONV_W = 4
NORM_EPS = 1e-6
L2_EPS = 1e-6
RW_GN_EPS = 64e-5
RW_G = RW_WIDTH // LANE
RW_PGROUPS = 4 * RW_G + 1
DN_PGROUPS = 4 * DN_HEADS + 1
P_GROUPS = RW_PGROUPS + DN_PGROUPS
VMEM_LIMIT = 56 * 1024 * 1024
IN_ROW_TILE = 1024
IN_COL_TILE = 768
OUT_ROW_TILE = 512
MIX_ROWS = 4


def _mm(a, b):
    return jnp.dot(a.astype(BF16), b.astype(BF16), preferred_element_type=F32)


def _mm_nt(a, b):
    return lax.dot_general(a.astype(BF16), b.astype(BF16), (((1,), (1,)), ((), ())),
                           preferred_element_type=F32)


def _mm_tn(a, b):
    return lax.dot_general(a.astype(BF16), b.astype(BF16), (((0,), (0,)), ((), ())),
                           preferred_element_type=F32)


def _each(fn, *lists):
    return [fn(*args) for args in zip(*lists)]


def _cumsum_rows(tri, x):
    hi = x.astype(BF16)
    lo = (x - hi.astype(F32)).astype(BF16)
    dot = lambda h: jnp.dot(tri, h, preferred_element_type=F32)
    return dot(hi) + dot(lo)


def _sigmoid(z):
    return 0.5 * jnp.tanh(0.5 * z) + 0.5


def _silu(z):
    h = 0.5 * z
    return h + h * jnp.tanh(h)


def _pair_time_index():
    t = lax.broadcasted_iota(jnp.int32, (CH, 2 * CH), 0)
    lane = lax.broadcasted_iota(jnp.int32, (CH, 2 * CH), 1)
    return t, lane & (CH - 1), lane < CH


def _block_diag(x, first):
    xb = x.astype(BF16)
    zero = jnp.zeros_like(xb)
    return jnp.concatenate([jnp.where(first, xb, zero), jnp.where(first, zero, xb)], axis=0)


def _tri_inverse_each(mats):
    i, j, first = _pair_time_index()
    eye = (i == j).astype(BF16)
    zero = jnp.zeros_like(eye)

    def off(m):
        return ((i & -(2 * m)) == (j & -(2 * m))) & ((i & m) != 0) & ((j & m) == 0)

    neg = [(-a).astype(BF16) for a in mats]
    ts = [jnp.where(off(1), na, eye) for na in neg]
    m = 2
    while m < CH:
        mask = off(m)
        prods = _each(lambda na, t: _mm(jnp.where(mask, na, zero), _block_diag(t, first)), neg, ts)
        corr = _each(lambda t, p: _mm(t, _block_diag(p, first)), ts, prods)
        ts = _each(lambda t, c: jnp.where(mask, c.astype(BF16), t), ts, corr)
        m *= 2
    return ts


def _same_head_mask():
    i2 = lax.broadcasted_iota(jnp.int32, (2 * CH, 2 * CH), 0)
    j2 = lax.broadcasted_iota(jnp.int32, (2 * CH, 2 * CH), 1)
    return (i2 >= CH) == (j2 >= CH)


def _tri_ones():
    ti = lax.broadcasted_iota(jnp.int32, (CH, CH), 0)
    tj = lax.broadcasted_iota(jnp.int32, (CH, CH), 1)
    return (ti >= tj).astype(BF16)


def _inproj_kernel(x_ref, nw_ref, wa_ref, wb_ref, o_ref, u_ref, *, n_sub, n_main):
    j = pl.program_id(1)

    @pl.when(j == 0)
    def _():
        x = x_ref[...]
        ms = jnp.mean(x * x, axis=-1, keepdims=True)
        u_ref[...] = (x * lax.rsqrt(ms + NORM_EPS) * nw_ref[...]).astype(BF16)

    def project(w_t):
        acc = lax.dot_general(u_ref[...], w_t, (((1,), (1,)), ((), ())),
                              preferred_element_type=F32)
        for s in range(n_sub):
            o_ref[s] = acc[:, s * LANE:(s + 1) * LANE]

    @pl.when(j < n_main)
    def _():
        project(wa_ref[...])

    @pl.when(j >= n_main)
    def _():
        project(wb_ref[...])


def _inproj(x2d, norm_w, w_main, w_tail, tm, tn):
    m, d = x2d.shape
    n_tail = w_tail.shape[0] // tn
    n_main = P_GROUPS * LANE // tn - n_tail
    n_sub = tn // LANE
    return pl.pallas_call(
        functools.partial(_inproj_kernel, n_sub=n_sub, n_main=n_main),
        grid=(m // tm, n_main + n_tail),
        in_specs=[
            pl.BlockSpec((tm, d), lambda i, j: (i, 0)),
            pl.BlockSpec((1, d), lambda i, j: (0, 0)),
            pl.BlockSpec((tn, d), lambda i, j: (jnp.minimum(j, n_main - 1), 0)),
            pl.BlockSpec((tn, d), lambda i, j: (jnp.maximum(j - n_main, 0), 0)),
        ],
        out_specs=pl.BlockSpec((n_sub, tm, LANE), lambda i, j: (j, i, 0)),
        out_shape=jax.ShapeDtypeStruct((P_GROUPS, m, LANE), F32),
        scratch_shapes=[pltpu.VMEM((tm, d), BF16)],
        compiler_params=pltpu.CompilerParams(
            dimension_semantics=("arbitrary", "arbitrary"), vmem_limit_bytes=VMEM_LIMIT),
        name="inproj",
    )(x2d, norm_w, w_main, w_tail)


def _rwkv_rows(p_ref, mu_ref, w0_ref, a0_ref, kk_ref, ka_ref, rk_ref, gnw_ref, gnb_ref,
               w2a2_ref, y_ref, s_ref, c_ref, *, row_ids):
    t_i, s_i, lo = _pair_time_index()
    strict = t_i > s_i
    incl = t_i >= s_i
    same_head = _same_head_mask()
    tri = _tri_ones()
    items = [(ri, g) for ri in row_ids for g in range(RW_G)]
    bd = lambda x: _block_diag(x, lo)

    def seg_sum(x):
        s_lo = jnp.sum(jnp.where(lo, x, 0.0), axis=-1, keepdims=True)
        s_hi = jnp.sum(jnp.where(lo, 0.0, x), axis=-1, keepdims=True)
        return jnp.where(lo, s_lo, s_hi)

    def shifted(group, ri, mu_idx, tail_idx):
        x = p_ref[group, ri]
        c_ref[ri, tail_idx, SUBLANE:, :] = x
        prev = c_ref[ri, tail_idx, SUBLANE - 1:SUBLANE - 1 + CH, :]
        return x + (prev - x) * mu_ref[mu_idx]

    loras = {}
    for ri in row_ids:
        lg = shifted(3 * RW_G, ri, 3 * RW_G, 3 * RW_G)
        xl = jnp.where(lo, jnp.tanh(lg), lg)
        loras[ri] = jnp.dot(xl.astype(BF16), w2a2_ref[...], preferred_element_type=F32)

    def mix_inputs(item):
        ri, g = item
        r = shifted(g, ri, g, g)
        k = shifted(RW_G + g, ri, RW_G + g, RW_G + g)
        v = shifted(2 * RW_G + g, ri, 2 * RW_G + g, 2 * RW_G + g)
        lw = -jnp.exp(-0.5) * _sigmoid(w0_ref[g] + loras[ri][:, g * LANE:(g + 1) * LANE])
        a = _sigmoid(a0_ref[g] + loras[ri][:, RW_WIDTH + g * LANE:RW_WIDTH + (g + 1) * LANE])
        kx = k * kk_ref[g]
        kk = kx * lax.rsqrt(seg_sum(kx * kx) + L2_EPS)
        return r, v, lw, kk, k * (1.0 + (a - 1.0) * ka_ref[g]), kk * a

    r_s, v_s, logw, kk_s, kmod, bb = zip(*[mix_inputs(item) for item in items])

    gam = []
    for n in range(len(row_ids)):
        wide = _cumsum_rows(tri, jnp.concatenate(logw[n * RW_G:(n + 1) * RW_G], axis=1))
        gam += [wide[:, g * LANE:(g + 1) * LANE] for g in range(RW_G)]

    def decay_scaled(x, lw, r, kk, k, b_):
        e_in = jnp.exp(x)
        e_inv = jnp.exp(-x)
        return (e_in, jnp.exp(x[CH - 1:CH, :] - x), r * e_in, kk * jnp.exp(x - lw),
                b_ * e_inv, k * e_inv)

    e_in, e_rem, rt, kt, b_inv, k_inv = zip(*_each(decay_scaled, gam, logw, r_s, kk_s, kmod, bb))

    a_all = _each(
        lambda kt_, rt_, b_, k_: _mm_nt(jnp.concatenate([kt_, rt_], axis=0),
                                        jnp.concatenate([bd(b_), bd(k_)], axis=0)),
        kt, rt, b_inv, k_inv)
    a_kb = [jnp.where(strict, a[:CH, :LANE], 0.0) for a in a_all]
    a_kk = [jnp.where(strict, a[:CH, LANE:], 0.0) for a in a_all]
    a_rb = [jnp.where(incl, a[CH:, :LANE], 0.0) for a in a_all]
    a_rk = [jnp.where(incl, a[CH:, LANE:], 0.0) for a in a_all]
    t_inv = yield a_kb

    av = _each(lambda kk_, rk_, v: _mm(jnp.concatenate([kk_, rk_], axis=0), bd(v)),
               a_kk, a_rk, v_s)
    yield
    tx = _each(lambda t, kt_, av_: _mm(t, jnp.concatenate([bd(kt_), bd(av_[:CH])], axis=1)),
               t_inv, kt, av)
    w_nat = [x[:, :LANE] for x in tx]
    u0 = [x[:, LANE:] for x in tx]
    y0 = [x[CH:] for x in av]

    yield
    s_old = [s_ref[ri, g] for ri, g in items]
    wh = _each(lambda w, r, s: _mm_nt(jnp.concatenate([w, r], axis=0), s), w_nat, rt, s_old)
    yield
    u = _each(lambda wh_, u0_: wh_[:CH] + u0_, wh, u0)
    arbu = _each(lambda a, u_: _mm(a, bd(u_)), a_rb, u)
    yield
    y = _each(lambda wh_, y0_, x: wh_[CH:] + y0_ - x, wh, y0, arbu)
    s_add = _each(
        lambda v, u_, k, b, e: _mm_tn(jnp.concatenate([v, u_], axis=0),
                                      jnp.concatenate([k * e, -(b * e)], axis=0)),
        v_s, u, kmod, bb, e_rem)
    for (ri, g), s, e, add in zip(items, s_old, e_in, s_add):
        s_ref[ri, g] = s * e[CH - 1:CH, :] + jnp.where(same_head, add, 0.0)

    yield
    mean = _each(lambda y_: seg_sum(y_) * (1.0 / RW_HEAD), y)
    dev = _each(lambda y_, m_: y_ - m_, y, mean)
    var = _each(lambda d: seg_sum(d * d) * (1.0 / RW_HEAD), dev)
    bonus = _each(lambda r, k, item: seg_sum(r * k * rk_ref[item[1]]), r_s, kmod, items)

    def finish(d, var_, bonus_, v, item):
        ri, g = item
        yn = d * lax.rsqrt(var_ + RW_GN_EPS) * gnw_ref[g] + gnb_ref[g] + bonus_ * v
        gate = p_ref[3 * RW_G + 1 + g, ri]
        y_ref[g, ri] = (yn * _silu(gate)).astype(y_ref.dtype)

    _each(finish, dev, var, bonus, v_s, items)


def _gdn_rows(p_ref, cw_ref, alog_ref, dtb_ref, nw_ref, y_ref, s_ref, c_ref, *, row_ids):
    t_i, s_i, lo = _pair_time_index()
    strict = t_i > s_i
    incl = t_i >= s_i
    lo_row = lo[:1]
    tri = _tri_ones()
    zero = jnp.zeros((CH, LANE), F32)
    heads = [(ri, h) for ri in row_ids for h in range(DN_HEADS)]
    pairs = [(n, ri, hp) for n, ri in enumerate(row_ids) for hp in range(DN_HEADS // 2)]
    first = lambda xs: xs[0::2]
    second = lambda xs: xs[1::2]

    def conv_silu(group, ri):
        x = p_ref[group, ri]
        c_ref[ri, group, SUBLANE:, :] = x
        acc = x * cw_ref[CONV_W - 1, group]
        for back in range(1, CONV_W):
            acc = acc + (c_ref[ri, group, SUBLANE - back:SUBLANE - back + CH, :]
                         * cw_ref[CONV_W - 1 - back, group])
        return _silu(acc)

    def l2n(x, scale=1.0):
        return x * (lax.rsqrt(jnp.sum(x * x, axis=-1, keepdims=True) + L2_EPS) * scale)

    beta_all, gc_all, gc_t = {}, {}, {}
    for ri in row_ids:
        ba = p_ref[4 * DN_HEADS, ri]
        beta_all[ri] = _sigmoid(ba)
        z = ba + dtb_ref[...]
        softplus = jnp.maximum(z, 0.0) + jnp.log1p(jnp.exp(-jnp.abs(z)))
        gc_all[ri] = _cumsum_rows(tri, -jnp.exp(alog_ref[...]) * softplus)
        gc_t[ri] = jnp.concatenate([gc_all[ri], gc_all[ri]], axis=0).T

    q = [l2n(conv_silu(h, ri), DN_HEAD ** -0.5) for ri, h in heads]
    k = [l2n(conv_silu(DN_HEADS + h, ri)) for ri, h in heads]
    v = [conv_silu(2 * DN_HEADS + h, ri) for ri, h in heads]
    beta = [beta_all[ri][:, h:h + 1] for ri, h in heads]
    gc = [gc_all[ri][:, DN_HEADS + h:DN_HEADS + h + 1] for ri, h in heads]
    kb = _each(lambda x, b: x * b, k, beta)
    vb = _each(lambda x, b: x * b, v, beta)
    e_gc = _each(jnp.exp, gc)
    qe = _each(lambda x, e: x * e, q, e_gc)
    g_last = [x[CH - 1:CH, :] for x in gc]

    def decay_mask(item):
        n, ri, hp = item
        col = jnp.where(lo, gc[n * DN_HEADS + 2 * hp], gc[n * DN_HEADS + 2 * hp + 1])
        r1 = DN_HEADS + 2 * hp
        rowv = jnp.where(lo_row, gc_t[ri][r1:r1 + 1, :], gc_t[ri][r1 + 1:r1 + 2, :])
        return jnp.where(incl, jnp.exp(col - rowv), 0.0)

    decay = _each(decay_mask, pairs)

    def diag2(a, b):
        return jnp.concatenate([jnp.concatenate([a, zero], axis=1),
                                jnp.concatenate([zero, b], axis=1)], axis=0)

    scores = _each(
        lambda kb1, kb2, q1, q2, k1, k2: _mm_nt(
            jnp.concatenate([jnp.concatenate([kb1, kb2], axis=1),
                             jnp.concatenate([q1, q2], axis=1)], axis=0), diag2(k1, k2)),
        first(kb), second(kb), first(q), second(q), first(k), second(k))
    m_mat = _each(lambda s, d: jnp.where(strict, s[:CH] * d, 0.0), scores, decay)
    attn = _each(lambda s, d: jnp.where(incl, s[CH:] * d, 0.0), scores, decay)
    t_inv = yield m_mat

    kbe = _each(lambda x, e: x * e, kb, e_gc)
    uw = _each(
        lambda t, vb1, vb2, kbe1, kbe2: _mm(t, jnp.concatenate([
            jnp.concatenate([vb1, kbe1, zero, zero], axis=1),
            jnp.concatenate([zero, zero, vb2, kbe2], axis=1)], axis=0)),
        t_inv, first(vb), second(vb), first(kbe), second(kbe))

    yield
    s_old = [s_ref[ri, h] for ri, h in heads]
    u_h = [uw[i // 2][:, (i % 2) * 2 * LANE:(i % 2) * 2 * LANE + LANE] for i in range(len(heads))]
    w_h = [uw[i // 2][:, (i % 2) * 2 * LANE + LANE:(i % 2 + 1) * 2 * LANE]
           for i in range(len(heads))]
    ws = _each(lambda w, qe_, s: _mm(jnp.concatenate([w, qe_], axis=0), s), w_h, qe, s_old)
    yield
    v_new = _each(lambda u_, ws_: u_ - ws_[:CH], u_h, ws)
    o2 = _each(lambda a, vn1, vn2: _mm(a, diag2(vn1, vn2)),
               attn, first(v_new), second(v_new))

    yield
    k_dec = _each(lambda x, gl, g: x * jnp.exp(gl - g), k, g_last, gc)
    s_add = _each(_mm_tn, k_dec, v_new)
    yield
    for i, (ri, h) in enumerate(heads):
        s_ref[ri, h] = s_old[i] * jnp.exp(g_last[i]) + s_add[i]
        o = ws[i][CH:] + o2[i // 2][:, (i % 2) * LANE:(i % 2 + 1) * LANE]
        o = o * lax.rsqrt(jnp.mean(o * o, axis=-1, keepdims=True) + NORM_EPS) * nw_ref[...]
        zg = p_ref[3 * DN_HEADS + h, ri]
        y_ref[h, ri] = (o * _silu(zg)).astype(y_ref.dtype)


def _mix_kernel(p_ref, mu_ref, w0_ref, a0_ref, kk_ref, ka_ref, rk_ref, gnw_ref, gnb_ref, w2a2_ref,
                cw_ref, alog_ref, dtb_ref, nw_ref, rw_cin_ref, rw_s0_ref, dn_cin_ref, dn_s0_ref,
                ya_ref, yb_ref, rw_cout_ref, rw_sout_ref, dn_cout_ref, dn_sout_ref,
                rw_s_ref, rw_c_ref, dn_s_ref, dn_c_ref, *, n_chunks, rows):
    c = pl.program_id(1)
    dn_p_ref = p_ref.at[RW_PGROUPS:P_GROUPS]

    @pl.when(c == 0)
    def _():
        for ri in range(rows):
            rw_s_ref[ri] = rw_s0_ref[...]
            rw_c_ref[ri, :, 0:SUBLANE, :] = rw_cin_ref[...]
            dn_s_ref[ri] = dn_s0_ref[...]
            dn_c_ref[ri, :, 0:SUBLANE, :] = dn_cin_ref[...]

    row_ids = tuple(range(rows))
    mixers = [
        _rwkv_rows(p_ref, mu_ref, w0_ref, a0_ref, kk_ref, ka_ref, rk_ref, gnw_ref, gnb_ref,
                   w2a2_ref, ya_ref, rw_s_ref, rw_c_ref, row_ids=row_ids),
        _gdn_rows(dn_p_ref, cw_ref, alog_ref, dtb_ref, nw_ref, yb_ref, dn_s_ref, dn_c_ref,
                  row_ids=row_ids),
    ]
    wanted = [next(mixer) for mixer in mixers]
    inverses = _tri_inverse_each([mat for mats in wanted for mat in mats])
    live = []
    for mixer, mats in zip(mixers, wanted):
        mixer.send(inverses[:len(mats)])
        inverses = inverses[len(mats):]
        live.append(mixer)
    while live:
        for mixer in list(live):
            try:
                next(mixer)
            except StopIteration:
                live.remove(mixer)

    for ri in range(rows):
        rw_c_ref[ri, :, 0:SUBLANE, :] = p_ref[0:3 * RW_G + 1, ri, CH - SUBLANE:CH, :]
        dn_c_ref[ri, :, 0:SUBLANE, :] = dn_p_ref[0:3 * DN_HEADS, ri, CH - SUBLANE:CH, :]

    @pl.when(c == n_chunks - 1)
    def _():
        rw_cout_ref[...] = rw_c_ref[rows - 1, :, 0:SUBLANE, :]
        rw_sout_ref[...] = rw_s_ref[rows - 1]
        dn_cout_ref[...] = dn_c_ref[rows - 1, :, 0:SUBLANE, :]
        dn_sout_ref[...] = dn_s_ref[rows - 1]


def _mix(p, rw_prm, dn_prm, rw_carry, rw_state, dn_carry, dn_state, rows):
    _, batch, seq, _ = p.shape
    n_chunks = seq // CH
    full = lambda shape: pl.BlockSpec(shape, lambda b, c: (0,) * len(shape))
    rw_carry_n = 3 * RW_G + 1
    dn_carry_n = 3 * DN_HEADS
    seq_block = lambda groups: pl.BlockSpec((groups, rows, CH, LANE), lambda b, c: (0, b, c, 0))
    rw_group = full((RW_G, 1, LANE))
    return pl.pallas_call(
        functools.partial(_mix_kernel, n_chunks=n_chunks, rows=rows),
        grid=(batch // rows, n_chunks),
        in_specs=[
            seq_block(P_GROUPS),
            full((rw_carry_n, 1, LANE)),
            rw_group, rw_group, rw_group, rw_group, rw_group, rw_group, rw_group,
            full((2 * RW_LORA, 2 * RW_WIDTH)),
            full((CONV_W, dn_carry_n, 1, LANE)),
            full((1, LANE)), full((1, LANE)), full((1, LANE)),
            full((rw_carry_n, SUBLANE, LANE)), full((RW_G, LANE, LANE)),
            full((dn_carry_n, SUBLANE, LANE)), full((DN_HEADS, DN_HEAD, DN_HEAD)),
        ],
        out_specs=[
            seq_block(RW_G), seq_block(DN_HEADS),
            full((rw_carry_n, SUBLANE, LANE)), full((RW_G, LANE, LANE)),
            full((dn_carry_n, SUBLANE, LANE)), full((DN_HEADS, DN_HEAD, DN_HEAD)),
        ],
        out_shape=[
            jax.ShapeDtypeStruct((RW_G, batch, seq, LANE), BF16),
            jax.ShapeDtypeStruct((DN_HEADS, batch, seq, LANE), BF16),
            jax.ShapeDtypeStruct((rw_carry_n, SUBLANE, LANE), F32),
            jax.ShapeDtypeStruct((RW_G, LANE, LANE), F32),
            jax.ShapeDtypeStruct((dn_carry_n, SUBLANE, LANE), F32),
            jax.ShapeDtypeStruct((DN_HEADS, DN_HEAD, DN_HEAD), F32),
        ],
        scratch_shapes=[
            pltpu.VMEM((rows, RW_G, LANE, LANE), F32),
            pltpu.VMEM((rows, rw_carry_n, SUBLANE + CH, LANE), F32),
            pltpu.VMEM((rows, DN_HEADS, DN_HEAD, DN_HEAD), F32),
            pltpu.VMEM((rows, dn_carry_n, SUBLANE + CH, LANE), F32),
        ],
        compiler_params=pltpu.CompilerParams(
            dimension_semantics=("arbitrary", "arbitrary"), vmem_limit_bytes=VMEM_LIMIT),
        name="mixers",
    )(p, rw_prm["mu"], rw_prm["w0"], rw_prm["a0"], rw_prm["k_k"], rw_prm["k_a"], rw_prm["r_k"],
      rw_prm["gn_w"], rw_prm["gn_b"], rw_prm["w2a2"], dn_prm["conv_w"], dn_prm["a_log"],
      dn_prm["dt_bias"], dn_prm["norm_w"], rw_carry, rw_state, dn_carry, dn_state)


def _outproj_kernel(ya_ref, yb_ref, x_ref, w_ref, fnw_ref, o_ref):
    y = jnp.concatenate([ya_ref[g] for g in range(RW_G)] + [yb_ref[h] for h in range(DN_HEADS)],
                        axis=1)
    hid = x_ref[...] + jnp.dot(y, w_ref[...], preferred_element_type=F32)
    ms = jnp.mean(hid * hid, axis=-1, keepdims=True)
    o_ref[...] = hid * lax.rsqrt(ms + NORM_EPS) * fnw_ref[...]


def _outproj(ya, yb, x2d, w_out, fnw, tm):
    m, d = x2d.shape
    return pl.pallas_call(
        _outproj_kernel,
        grid=(m // tm,),
        in_specs=[
            pl.BlockSpec((RW_G, tm, LANE), lambda i: (0, i, 0)),
            pl.BlockSpec((DN_HEADS, tm, LANE), lambda i: (0, i, 0)),
            pl.BlockSpec((tm, d), lambda i: (i, 0)),
            pl.BlockSpec((RW_WIDTH + DN_WIDTH, d), lambda i: (0, 0)),
            pl.BlockSpec((1, d), lambda i: (0, 0)),
        ],
        out_specs=pl.BlockSpec((tm, d), lambda i: (i, 0)),
        out_shape=jax.ShapeDtypeStruct((m, d), F32),
        compiler_params=pltpu.CompilerParams(
            dimension_semantics=("arbitrary",), vmem_limit_bytes=VMEM_LIMIT),
        name="outproj",
    )(ya, yb, x2d, w_out, fnw)


def _row_tile(m, cap):
    t = cap
    while m % t:
        t //= 2
    return t


def kernel(x, meta_tokens, norm_w, w_in, rw_shift_mu, rw_w0, rw_w2, rw_a0, rw_a2, rw_k_k, rw_k_a,
           rw_r_k, rw_gn_w, rw_gn_b, dn_conv_w, dn_A_log, dn_dt_bias, dn_norm_w, w_out,
           final_norm_w):
    batch, seq, d = x.shape
    assert seq % CH == 0 and norm_w.shape[0] == 1
    m = batch * seq

    tn = IN_COL_TILE
    wi = w_in[0].T.astype(BF16)
    c_b = 3 * RW_WIDTH + 2 * RW_LORA + RW_WIDTH + 3 * DN_WIDTH
    c_z = c_b + 2 * DN_HEADS
    c_tail = (P_GROUPS * LANE // tn - 2) * tn
    assert c_tail <= c_b and c_tail % LANE == 0
    w_tail = jnp.concatenate([
        wi[c_tail:c_b], wi[c_z:], wi[c_b:c_z],
        jnp.zeros((LANE - 2 * DN_HEADS, d), wi.dtype)], axis=0)
    grp = lambda t: t.reshape(-1, 1, LANE)
    zeros_l = jnp.zeros((RW_LORA, RW_WIDTH), F32)
    rw_prm = {
        "mu": grp(rw_shift_mu[0]),
        "w0": grp(rw_w0[0]), "a0": grp(rw_a0[0]), "k_k": grp(rw_k_k[0]), "k_a": grp(rw_k_a[0]),
        "r_k": grp(rw_r_k[0]), "gn_w": grp(rw_gn_w[0]), "gn_b": grp(rw_gn_b[0]),
        "w2a2": jnp.concatenate([jnp.concatenate([rw_w2[0], zeros_l], axis=1),
                                 jnp.concatenate([zeros_l, rw_a2[0]], axis=1)], axis=0).astype(BF16),
    }
    lane_vec = lambda t: jnp.zeros((1, LANE), F32).at[0, DN_HEADS:2 * DN_HEADS].set(t)
    dn_prm = {
        "conv_w": dn_conv_w[0].reshape(CONV_W, 3 * DN_HEADS, 1, LANE),
        "a_log": lane_vec(dn_A_log[0]), "dt_bias": lane_vec(dn_dt_bias[0]),
        "norm_w": dn_norm_w[0].reshape(1, LANE),
    }
    nw = norm_w[0].reshape(1, d)

    meta_rows = jnp.concatenate([jnp.zeros((CH - N_META, d), x.dtype), meta_tokens.astype(x.dtype)],
                                axis=0)
    p_meta = _inproj(meta_rows, nw, wi, w_tail, CH, tn).reshape(P_GROUPS, 1, CH, LANE)
    zc = lambda n: jnp.zeros((n, SUBLANE, LANE), F32)
    _, _, rw_carry, rw_state, dn_carry, dn_state = _mix(
        p_meta, rw_prm, dn_prm, zc(3 * RW_G + 1), jnp.zeros((RW_G, LANE, LANE), F32),
        zc(3 * DN_HEADS), jnp.zeros((DN_HEADS, DN_HEAD, DN_HEAD), F32), 1)

    x2d = x.reshape(m, d)
    p = _inproj(x2d, nw, wi, w_tail, _row_tile(m, IN_ROW_TILE), tn)
    p = p.reshape(P_GROUPS, batch, seq, LANE)
    ya, yb, _, _, _, _ = _mix(p, rw_prm, dn_prm, rw_carry, rw_state, dn_carry, dn_state,
                              MIX_ROWS if batch % MIX_ROWS == 0 else 1)
    out = _outproj(ya.reshape(RW_G, m, LANE), yb.reshape(DN_HEADS, m, LANE), x2d,
                   w_out[0].astype(BF16), final_norm_w.reshape(1, d), _row_tile(m, OUT_ROW_TILE))
    return out.reshape(batch, seq, d)
```

```python
import functools

import jax
import jax.numpy as jnp
from jax import lax
from jax.experimental import pallas as pl
from jax.experimental.pallas import tpu as pltpu

F32 = jnp.float32
BF16 = jnp.bfloat16

LANE = 128
SUBLANE = 8
CH = 64
N_META = 16
RW_WIDTH = 1024
RW_HEAD = 64
RW_LORA = 64
DN_WIDTH = 1024
DN_HEAD = 128
DN_HEADS = DN_WIDTH // DN_HEAD
CONV_W = 4
NORM_EPS = 1e-6
L2_EPS = 1e-6
RW_GN_EPS = 64e-5
RW_G = RW_WIDTH // LANE
RW_PGROUPS = 4 * RW_G + 1
DN_PGROUPS = 4 * DN_HEADS + 1
P_GROUPS = RW_PGROUPS + DN_PGROUPS
VMEM_LIMIT = 56 * 1024 * 1024
IN_ROW_TILE = 1024
IN_COL_TILE = 768
OUT_ROW_TILE = 512
MIX_ROWS = 4


def _mm(a, b):
    return jnp.dot(a.astype(BF16), b.astype(BF16), preferred_element_type=F32)


def _mm_nt(a, b):
    return lax.dot_general(a.astype(BF16), b.astype(BF16), (((1,), (1,)), ((), ())),
                           preferred_element_type=F32)


def _mm_tn(a, b):
    return lax.dot_general(a.astype(BF16), b.astype(BF16), (((0,), (0,)), ((), ())),
                           preferred_element_type=F32)


def _each(fn, *lists):
    return [fn(*args) for args in zip(*lists)]


def _cumsum_rows(tri, x):
    hi = x.astype(BF16)
    lo = (x - hi.astype(F32)).astype(BF16)
    dot = lambda h: jnp.dot(tri, h, preferred_element_type=F32)
    return dot(hi) + dot(lo)


def _sigmoid(z):
    return 0.5 * jnp.tanh(0.5 * z) + 0.5


def _silu(z):
    h = 0.5 * z
    return h + h * jnp.tanh(h)


def _pair_time_index():
    t = lax.broadcasted_iota(jnp.int32, (CH, 2 * CH), 0)
    lane = lax.broadcasted_iota(jnp.int32, (CH, 2 * CH), 1)
    return t, lane & (CH - 1), lane < CH


def _block_diag(x, first):
    xb = x.astype(BF16)
    zero = jnp.zeros_like(xb)
    return jnp.concatenate([jnp.where(first, xb, zero), jnp.where(first, zero, xb)], axis=0)


def _tri_inverse_each(mats):
    i, j, first = _pair_time_index()
    eye = (i == j).astype(BF16)
    zero = jnp.zeros_like(eye)

    def off(m):
        return ((i & -(2 * m)) == (j & -(2 * m))) & ((i & m) != 0) & ((j & m) == 0)

    neg = [(-a).astype(BF16) for a in mats]
    ts = [jnp.where(off(1), na, eye) for na in neg]
    m = 2
    while m < CH:
        mask = off(m)
        prods = _each(lambda na, t: _mm(jnp.where(mask, na, zero), _block_diag(t, first)), neg, ts)
        corr = _each(lambda t, p: _mm(t, _block_diag(p, first)), ts, prods)
        ts = _each(lambda t, c: jnp.where(mask, c.astype(BF16), t), ts, corr)
        m *= 2
    return ts


def _same_head_mask():
    i2 = lax.broadcasted_iota(jnp.int32, (2 * CH, 2 * CH), 0)
    j2 = lax.broadcasted_iota(jnp.int32, (2 * CH, 2 * CH), 1)
    return (i2 >= CH) == (j2 >= CH)


def _tri_ones():
    ti = lax.broadcasted_iota(jnp.int32, (CH, CH), 0)
    tj = lax.broadcasted_iota(jnp.int32, (CH, CH), 1)
    return (ti >= tj).astype(BF16)


def _inproj_kernel(x_ref, xm_ref, nw_ref, wa_ref, wb_ref, o_ref, om_ref, u_ref, um_ref,
                   *, n_sub, n_main):
    i = pl.program_id(0)
    j = pl.program_id(1)

    def normed(x):
        ms = jnp.mean(x * x, axis=-1, keepdims=True)
        return (x * lax.rsqrt(ms + NORM_EPS) * nw_ref[...]).astype(BF16)

    @pl.when(j == 0)
    def _():
        u_ref[...] = normed(x_ref[...])

    @pl.when((j == 0) & (i == 0))
    def _():
        um_ref[...] = normed(xm_ref[...])

    def project(u, w_t, out_ref):
        acc = lax.dot_general(u, w_t, (((1,), (1,)), ((), ())), preferred_element_type=F32)
        for s in range(n_sub):
            out_ref[s] = acc[:, s * LANE:(s + 1) * LANE]

    def step(w_ref):
        project(u_ref[...], w_ref[...], o_ref)

        @pl.when(i == 0)
        def _():
            project(um_ref[...], w_ref[...], om_ref)

    @pl.when(j < n_main)
    def _():
        step(wa_ref)

    @pl.when(j >= n_main)
    def _():
        step(wb_ref)


def _inproj(x2d, x_meta, norm_w, w_main, w_tail, tm, tn):
    m, d = x2d.shape
    n_tail = w_tail.shape[0] // tn
    n_col = P_GROUPS * LANE // tn
    n_main = n_col - n_tail
    n_sub = tn // LANE
    return pl.pallas_call(
        functools.partial(_inproj_kernel, n_sub=n_sub, n_main=n_main),
        grid=(m // tm, n_col),
        in_specs=[
            pl.BlockSpec((tm, d), lambda i, j: (i, 0)),
            pl.BlockSpec((CH, d), lambda i, j: (0, 0)),
            pl.BlockSpec((1, d), lambda i, j: (0, 0)),
            pl.BlockSpec((tn, d), lambda i, j: (jnp.minimum(j, n_main - 1), 0)),
            pl.BlockSpec((tn, d), lambda i, j: (jnp.maximum(j - n_main, 0), 0)),
        ],
        out_specs=[
            pl.BlockSpec((n_sub, tm, LANE), lambda i, j: (j, i, 0)),
            pl.BlockSpec((n_sub, CH, LANE), lambda i, j: (jnp.where(i == 0, j, n_col - 1), 0, 0)),
        ],
        out_shape=[jax.ShapeDtypeStruct((P_GROUPS, m, LANE), F32),
                   jax.ShapeDtypeStruct((P_GROUPS, CH, LANE), F32)],
        scratch_shapes=[pltpu.VMEM((tm, d), BF16), pltpu.VMEM((CH, d), BF16)],
        compiler_params=pltpu.CompilerParams(
            dimension_semantics=("arbitrary", "arbitrary"), vmem_limit_bytes=VMEM_LIMIT),
        name="inproj",
    )(x2d, x_meta, norm_w, w_main, w_tail)


def _rwkv_rows(p_ref, mu_ref, w0_ref, a0_ref, kk_ref, ka_ref, rk_ref, gnw_ref, gnb_ref,
               w2a2_ref, y_ref, s_ref, c_ref, *, row_ids):
    t_i, s_i, lo = _pair_time_index()
    strict = t_i > s_i
    incl = t_i >= s_i
    same_head = _same_head_mask()
    tri = _tri_ones()
    items = [(ri, g) for ri in row_ids for g in range(RW_G)]
    bd = lambda x: _block_diag(x, lo)

    def seg_sum(x):
        s_lo = jnp.sum(jnp.where(lo, x, 0.0), axis=-1, keepdims=True)
        s_hi = jnp.sum(jnp.where(lo, 0.0, x), axis=-1, keepdims=True)
        return jnp.where(lo, s_lo, s_hi)

    def shifted(group, ri, mu_idx, tail_idx):
        x = p_ref[group, ri]
        c_ref[ri, tail_idx, SUBLANE:, :] = x
        prev = c_ref[ri, tail_idx, SUBLANE - 1:SUBLANE - 1 + CH, :]
        return x + (prev - x) * mu_ref[mu_idx]

    loras = {}
    for ri in row_ids:
        lg = shifted(3 * RW_G, ri, 3 * RW_G, 3 * RW_G)
        xl = jnp.where(lo, jnp.tanh(lg), lg)
        loras[ri] = jnp.dot(xl.astype(BF16), w2a2_ref[...], preferred_element_type=F32)

    def mix_inputs(item):
        ri, g = item
        r = shifted(g, ri, g, g)
        k = shifted(RW_G + g, ri, RW_G + g, RW_G + g)
        v = shifted(2 * RW_G + g, ri, 2 * RW_G + g, 2 * RW_G + g)
        lw = -jnp.exp(-0.5) * _sigmoid(w0_ref[g] + loras[ri][:, g * LANE:(g + 1) * LANE])
        a = _sigmoid(a0_ref[g] + loras[ri][:, RW_WIDTH + g * LANE:RW_WIDTH + (g + 1) * LANE])
        kx = k * kk_ref[g]
        kk = kx * lax.rsqrt(seg_sum(kx * kx) + L2_EPS)
        return r, v, lw, kk, k * (1.0 + (a - 1.0) * ka_ref[g]), kk * a

    r_s, v_s, logw, kk_s, kmod, bb = zip(*[mix_inputs(item) for item in items])

    gam = []
    for n in range(len(row_ids)):
        wide = _cumsum_rows(tri, jnp.concatenate(logw[n * RW_G:(n + 1) * RW_G], axis=1))
        gam += [wide[:, g * LANE:(g + 1) * LANE] for g in range(RW_G)]

    def decay_scaled(x, lw, r, kk, k, b_):
        e_in = jnp.exp(x)
        e_inv = jnp.exp(-x)
        return (e_in, jnp.exp(x[CH - 1:CH, :] - x), r * e_in, kk * jnp.exp(x - lw),
                b_ * e_inv, k * e_inv)

    e_in, e_rem, rt, kt, b_inv, k_inv = zip(*_each(decay_scaled, gam, logw, r_s, kk_s, kmod, bb))

    a_all = _each(
        lambda kt_, rt_, b_, k_: _mm_nt(jnp.concatenate([kt_, rt_], axis=0),
                                        jnp.concatenate([bd(b_), bd(k_)], axis=0)),
        kt, rt, b_inv, k_inv)
    a_kb = [jnp.where(strict, a[:CH, :LANE], 0.0) for a in a_all]
    a_kk = [jnp.where(strict, a[:CH, LANE:], 0.0) for a in a_all]
    a_rb = [jnp.where(incl, a[CH:, :LANE], 0.0) for a in a_all]
    a_rk = [jnp.where(incl, a[CH:, LANE:], 0.0) for a in a_all]
    t_inv = yield a_kb

    av = _each(lambda kk_, rk_, v: _mm(jnp.concatenate([kk_, rk_], axis=0), bd(v)),
               a_kk, a_rk, v_s)
    yield
    tx = _each(lambda t, kt_, av_: _mm(t, jnp.concatenate([bd(kt_), bd(av_[:CH])], axis=1)),
               t_inv, kt, av)
    w_nat = [x[:, :LANE] for x in tx]
    u0 = [x[:, LANE:] for x in tx]
    y0 = [x[CH:] for x in av]

    yield
    s_old = [s_ref[ri, g] for ri, g in items]
    wh = _each(lambda w, r, s: _mm_nt(jnp.concatenate([w, r], axis=0), s), w_nat, rt, s_old)
    yield
    u = _each(lambda wh_, u0_: wh_[:CH] + u0_, wh, u0)
    arbu = _each(lambda a, u_: _mm(a, bd(u_)), a_rb, u)
    yield
    y = _each(lambda wh_, y0_, x: wh_[CH:] + y0_ - x, wh, y0, arbu)
    s_add = _each(
        lambda v, u_, k, b, e: _mm_tn(jnp.concatenate([v, u_], axis=0),
                                      jnp.concatenate([k * e, -(b * e)], axis=0)),
        v_s, u, kmod, bb, e_rem)
    for (ri, g), s, e, add in zip(items, s_old, e_in, s_add):
        s_ref[ri, g] = s * e[CH - 1:CH, :] + jnp.where(same_head, add, 0.0)

    yield
    mean = _each(lambda y_: seg_sum(y_) * (1.0 / RW_HEAD), y)
    dev = _each(lambda y_, m_: y_ - m_, y, mean)
    var = _each(lambda d: seg_sum(d * d) * (1.0 / RW_HEAD), dev)
    bonus = _each(lambda r, k, item: seg_sum(r * k * rk_ref[item[1]]), r_s, kmod, items)

    def finish(d, var_, bonus_, v, item):
        ri, g = item
        yn = d * lax.rsqrt(var_ + RW_GN_EPS) * gnw_ref[g] + gnb_ref[g] + bonus_ * v
        gate = p_ref[3 * RW_G + 1 + g, ri]
        y_ref[g, ri] = (yn * _silu(gate)).astype(y_ref.dtype)

    _each(finish, dev, var, bonus, v_s, items)


def _gdn_rows(p_ref, cw_ref, alog_ref, dtb_ref, nw_ref, y_ref, s_ref, c_ref, *, row_ids):
    t_i, s_i, lo = _pair_time_index()
    strict = t_i > s_i
    incl = t_i >= s_i
    lo_row = lo[:1]
    tri = _tri_ones()
    zero = jnp.zeros((CH, LANE), F32)
    heads = [(ri, h) for ri in row_ids for h in range(DN_HEADS)]
    pairs = [(n, ri, hp) for n, ri in enumerate(row_ids) for hp in range(DN_HEADS // 2)]
    first = lambda xs: xs[0::2]
    second = lambda xs: xs[1::2]

    def conv_silu(group, ri):
        x = p_ref[group, ri]
        c_ref[ri, group, SUBLANE:, :] = x
        acc = x * cw_ref[CONV_W - 1, group]
        for back in range(1, CONV_W):
            acc = acc + (c_ref[ri, group, SUBLANE - back:SUBLANE - back + CH, :]
                         * cw_ref[CONV_W - 1 - back, group])
        return _silu(acc)

    def l2n(x, scale=1.0):
        return x * (lax.rsqrt(jnp.sum(x * x, axis=-1, keepdims=True) + L2_EPS) * scale)

    beta_all, gc_all, gc_t = {}, {}, {}
    for ri in row_ids:
        ba = p_ref[4 * DN_HEADS, ri]
        beta_all[ri] = _sigmoid(ba)
        z = ba + dtb_ref[...]
        softplus = jnp.maximum(z, 0.0) + jnp.log1p(jnp.exp(-jnp.abs(z)))
        gc_all[ri] = _cumsum_rows(tri, -jnp.exp(alog_ref[...]) * softplus)
        gc_t[ri] = jnp.concatenate([gc_all[ri], gc_all[ri]], axis=0).T

    q = [l2n(conv_silu(h, ri), DN_HEAD ** -0.5) for ri, h in heads]
    k = [l2n(conv_silu(DN_HEADS + h, ri)) for ri, h in heads]
    v = [conv_silu(2 * DN_HEADS + h, ri) for ri, h in heads]
    beta = [beta_all[ri][:, h:h + 1] for ri, h in heads]
    gc = [gc_all[ri][:, DN_HEADS + h:DN_HEADS + h + 1] for ri, h in heads]
    kb = _each(lambda x, b: x * b, k, beta)
    vb = _each(lambda x, b: x * b, v, beta)
    e_gc = _each(jnp.exp, gc)
    qe = _each(lambda x, e: x * e, q, e_gc)
    g_last = [x[CH - 1:CH, :] for x in gc]

    def decay_mask(item):
        n, ri, hp = item
        col = jnp.where(lo, gc[n * DN_HEADS + 2 * hp], gc[n * DN_HEADS + 2 * hp + 1])
        r1 = DN_HEADS + 2 * hp
        rowv = jnp.where(lo_row, gc_t[ri][r1:r1 + 1, :], gc_t[ri][r1 + 1:r1 + 2, :])
        return jnp.where(incl, jnp.exp(col - rowv), 0.0)

    decay = _each(decay_mask, pairs)

    def diag2(a, b):
        return jnp.concatenate([jnp.concatenate([a, zero], axis=1),
                                jnp.concatenate([zero, b], axis=1)], axis=0)

    scores = _each(
        lambda kb1, kb2, q1, q2, k1, k2: _mm_nt(
            jnp.concatenate([jnp.concatenate([kb1, kb2], axis=1),
                             jnp.concatenate([q1, q2], axis=1)], axis=0), diag2(k1, k2)),
        first(kb), second(kb), first(q), second(q), first(k), second(k))
    m_mat = _each(lambda s, d: jnp.where(strict, s[:CH] * d, 0.0), scores, decay)
    attn = _each(lambda s, d: jnp.where(incl, s[CH:] * d, 0.0), scores, decay)
    t_inv = yield m_mat

    kbe = _each(lambda x, e: x * e, kb, e_gc)
    uw = _each(
        lambda t, vb1, vb2, kbe1, kbe2: _mm(t, jnp.concatenate([
            jnp.concatenate([vb1, kbe1, zero, zero], axis=1),
            jnp.concatenate([zero, zero, vb2, kbe2], axis=1)], axis=0)),
        t_inv, first(vb), second(vb), first(kbe), second(kbe))

    yield
    s_old = [s_ref[ri, h] for ri, h in heads]
    u_h = [uw[i // 2][:, (i % 2) * 2 * LANE:(i % 2) * 2 * LANE + LANE] for i in range(len(heads))]
    w_h = [uw[i // 2][:, (i % 2) * 2 * LANE + LANE:(i % 2 + 1) * 2 * LANE]
           for i in range(len(heads))]
    ws = _each(lambda w, qe_, s: _mm(jnp.concatenate([w, qe_], axis=0), s), w_h, qe, s_old)
    yield
    v_new = _each(lambda u_, ws_: u_ - ws_[:CH], u_h, ws)
    o2 = _each(lambda a, vn1, vn2: _mm(a, diag2(vn1, vn2)),
               attn, first(v_new), second(v_new))

    yield
    k_dec = _each(lambda x, gl, g: x * jnp.exp(gl - g), k, g_last, gc)
    s_add = _each(_mm_tn, k_dec, v_new)
    yield
    for i, (ri, h) in enumerate(heads):
        s_ref[ri, h] = s_old[i] * jnp.exp(g_last[i]) + s_add[i]
        o = ws[i][CH:] + o2[i // 2][:, (i % 2) * LANE:(i % 2 + 1) * LANE]
        o = o * lax.rsqrt(jnp.mean(o * o, axis=-1, keepdims=True) + NORM_EPS) * nw_ref[...]
        zg = p_ref[3 * DN_HEADS + h, ri]
        y_ref[h, ri] = (o * _silu(zg)).astype(y_ref.dtype)


def _mix_kernel(p_ref, mu_ref, w0_ref, a0_ref, kk_ref, ka_ref, rk_ref, gnw_ref, gnb_ref, w2a2_ref,
                cw_ref, alog_ref, dtb_ref, nw_ref, rw_cin_ref, rw_s0_ref, dn_cin_ref, dn_s0_ref,
                ya_ref, yb_ref, rw_cout_ref, rw_sout_ref, dn_cout_ref, dn_sout_ref,
                rw_s_ref, rw_c_ref, dn_s_ref, dn_c_ref, *, n_chunks, rows):
    c = pl.program_id(1)
    dn_p_ref = p_ref.at[RW_PGROUPS:P_GROUPS]

    @pl.when(c == 0)
    def _():
        for ri in range(rows):
            rw_s_ref[ri] = rw_s0_ref[...]
            rw_c_ref[ri, :, 0:SUBLANE, :] = rw_cin_ref[...]
            dn_s_ref[ri] = dn_s0_ref[...]
            dn_c_ref[ri, :, 0:SUBLANE, :] = dn_cin_ref[...]

    row_ids = tuple(range(rows))
    mixers = [
        _rwkv_rows(p_ref, mu_ref, w0_ref, a0_ref, kk_ref, ka_ref, rk_ref, gnw_ref, gnb_ref,
                   w2a2_ref, ya_ref, rw_s_ref, rw_c_ref, row_ids=row_ids),
        _gdn_rows(dn_p_ref, cw_ref, alog_ref, dtb_ref, nw_ref, yb_ref, dn_s_ref, dn_c_ref,
                  row_ids=row_ids),
    ]
    wanted = [next(mixer) for mixer in mixers]
    inverses = _tri_inverse_each([mat for mats in wanted for mat in mats])
    live = []
    for mixer, mats in zip(mixers, wanted):
        mixer.send(inverses[:len(mats)])
        inverses = inverses[len(mats):]
        live.append(mixer)
    while live:
        for mixer in list(live):
            try:
                next(mixer)
            except StopIteration:
                live.remove(mixer)

    for ri in range(rows):
        rw_c_ref[ri, :, 0:SUBLANE, :] = p_ref[0:3 * RW_G + 1, ri, CH - SUBLANE:CH, :]
        dn_c_ref[ri, :, 0:SUBLANE, :] = dn_p_ref[0:3 * DN_HEADS, ri, CH - SUBLANE:CH, :]

    @pl.when(c == n_chunks - 1)
    def _():
        rw_cout_ref[...] = rw_c_ref[rows - 1, :, 0:SUBLANE, :]
        rw_sout_ref[...] = rw_s_ref[rows - 1]
        dn_cout_ref[...] = dn_c_ref[rows - 1, :, 0:SUBLANE, :]
        dn_sout_ref[...] = dn_s_ref[rows - 1]


def _mix(p, rw_prm, dn_prm, rw_carry, rw_state, dn_carry, dn_state, rows):
    _, batch, seq, _ = p.shape
    n_chunks = seq // CH
    full = lambda shape: pl.BlockSpec(shape, lambda b, c: (0,) * len(shape))
    rw_carry_n = 3 * RW_G + 1
    dn_carry_n = 3 * DN_HEADS
    seq_block = lambda groups: pl.BlockSpec((groups, rows, CH, LANE), lambda b, c: (0, b, c, 0))
    rw_group = full((RW_G, 1, LANE))
    return pl.pallas_call(
        functools.partial(_mix_kernel, n_chunks=n_chunks, rows=rows),
        grid=(batch // rows, n_chunks),
        in_specs=[
            seq_block(P_GROUPS),
            full((rw_carry_n, 1, LANE)),
            rw_group, rw_group, rw_group, rw_group, rw_group, rw_group, rw_group,
            full((2 * RW_LORA, 2 * RW_WIDTH)),
            full((CONV_W, dn_carry_n, 1, LANE)),
            full((1, LANE)), full((1, LANE)), full((1, LANE)),
            full((rw_carry_n, SUBLANE, LANE)), full((RW_G, LANE, LANE)),
            full((dn_carry_n, SUBLANE, LANE)), full((DN_HEADS, DN_HEAD, DN_HEAD)),
        ],
        out_specs=[
            seq_block(RW_G), seq_block(DN_HEADS),
            full((rw_carry_n, SUBLANE, LANE)), full((RW_G, LANE, LANE)),
            full((dn_carry_n, SUBLANE, LANE)), full((DN_HEADS, DN_HEAD, DN_HEAD)),
        ],
        out_shape=[
            jax.ShapeDtypeStruct((RW_G, batch, seq, LANE), BF16),
            jax.ShapeDtypeStruct((DN_HEADS, batch, seq, LANE), BF16),
            jax.ShapeDtypeStruct((rw_carry_n, SUBLANE, LANE), F32),
            jax.ShapeDtypeStruct((RW_G, LANE, LANE), F32),
            jax.ShapeDtypeStruct((dn_carry_n, SUBLANE, LANE), F32),
            jax.ShapeDtypeStruct((DN_HEADS, DN_HEAD, DN_HEAD), F32),
        ],
        scratch_shapes=[
            pltpu.VMEM((rows, RW_G, LANE, LANE), F32),
            pltpu.VMEM((rows, rw_carry_n, SUBLANE + CH, LANE), F32),
            pltpu.VMEM((rows, DN_HEADS, DN_HEAD, DN_HEAD), F32),
            pltpu.VMEM((rows, dn_carry_n, SUBLANE + CH, LANE), F32),
        ],
        compiler_params=pltpu.CompilerParams(
            dimension_semantics=("arbitrary", "arbitrary"), vmem_limit_bytes=VMEM_LIMIT),
        name="mixers",
    )(p, rw_prm["mu"], rw_prm["w0"], rw_prm["a0"], rw_prm["k_k"], rw_prm["k_a"], rw_prm["r_k"],
      rw_prm["gn_w"], rw_prm["gn_b"], rw_prm["w2a2"], dn_prm["conv_w"], dn_prm["a_log"],
      dn_prm["dt_bias"], dn_prm["norm_w"], rw_carry, rw_state, dn_carry, dn_state)


def _outproj_kernel(ya_ref, yb_ref, x_ref, w_ref, fnw_ref, o_ref):
    y = jnp.concatenate([ya_ref[g] for g in range(RW_G)] + [yb_ref[h] for h in range(DN_HEADS)],
                        axis=1)
    hid = x_ref[...] + jnp.dot(y, w_ref[...], preferred_element_type=F32)
    ms = jnp.mean(hid * hid, axis=-1, keepdims=True)
    o_ref[...] = hid * lax.rsqrt(ms + NORM_EPS) * fnw_ref[...]


def _outproj(ya, yb, x2d, w_out, fnw, tm):
    m, d = x2d.shape
    return pl.pallas_call(
        _outproj_kernel,
        grid=(m // tm,),
        in_specs=[
            pl.BlockSpec((RW_G, tm, LANE), lambda i: (0, i, 0)),
            pl.BlockSpec((DN_HEADS, tm, LANE), lambda i: (0, i, 0)),
            pl.BlockSpec((tm, d), lambda i: (i, 0)),
            pl.BlockSpec((RW_WIDTH + DN_WIDTH, d), lambda i: (0, 0)),
            pl.BlockSpec((1, d), lambda i: (0, 0)),
        ],
        out_specs=pl.BlockSpec((tm, d), lambda i: (i, 0)),
        out_shape=jax.ShapeDtypeStruct((m, d), F32),
        compiler_params=pltpu.CompilerParams(
            dimension_semantics=("arbitrary",), vmem_limit_bytes=VMEM_LIMIT),
        name="outproj",
    )(ya, yb, x2d, w_out, fnw)


def _row_tile(m, cap):
    t = cap
    while m % t:
        t //= 2
    return t


def kernel(x, meta_tokens, norm_w, w_in, rw_shift_mu, rw_w0, rw_w2, rw_a0, rw_a2, rw_k_k, rw_k_a,
           rw_r_k, rw_gn_w, rw_gn_b, dn_conv_w, dn_A_log, dn_dt_bias, dn_norm_w, w_out,
           final_norm_w):
    batch, seq, d = x.shape
    assert seq % CH == 0 and norm_w.shape[0] == 1
    m = batch * seq

    tn = IN_COL_TILE
    wi = w_in[0].T.astype(BF16)
    c_b = 3 * RW_WIDTH + 2 * RW_LORA + RW_WIDTH + 3 * DN_WIDTH
    c_z = c_b + 2 * DN_HEADS
    c_tail = (P_GROUPS * LANE // tn - 2) * tn
    assert c_tail <= c_b and c_tail % LANE == 0
    w_tail = jnp.concatenate([
        wi[c_tail:c_b], wi[c_z:], wi[c_b:c_z],
        jnp.zeros((LANE - 2 * DN_HEADS, d), wi.dtype)], axis=0)
    grp = lambda t: t.reshape(-1, 1, LANE)
    zeros_l = jnp.zeros((RW_LORA, RW_WIDTH), F32)
    rw_prm = {
        "mu": grp(rw_shift_mu[0]),
        "w0": grp(rw_w0[0]), "a0": grp(rw_a0[0]), "k_k": grp(rw_k_k[0]), "k_a": grp(rw_k_a[0]),
        "r_k": grp(rw_r_k[0]), "gn_w": grp(rw_gn_w[0]), "gn_b": grp(rw_gn_b[0]),
        "w2a2": jnp.concatenate([jnp.concatenate([rw_w2[0], zeros_l], axis=1),
                                 jnp.concatenate([zeros_l, rw_a2[0]], axis=1)], axis=0).astype(BF16),
    }
    lane_vec = lambda t: jnp.zeros((1, LANE), F32).at[0, DN_HEADS:2 * DN_HEADS].set(t)
    dn_prm = {
        "conv_w": dn_conv_w[0].reshape(CONV_W, 3 * DN_HEADS, 1, LANE),
        "a_log": lane_vec(dn_A_log[0]), "dt_bias": lane_vec(dn_dt_bias[0]),
        "norm_w": dn_norm_w[0].reshape(1, LANE),
    }
    nw = norm_w[0].reshape(1, d)

    meta_rows = jnp.concatenate([jnp.zeros((CH - N_META, d), x.dtype), meta_tokens.astype(x.dtype)],
                                axis=0)
    x2d = x.reshape(m, d)
    p, p_meta = _inproj(x2d, meta_rows, nw, wi, w_tail, _row_tile(m, IN_ROW_TILE), tn)
    zc = lambda n: jnp.zeros((n, SUBLANE, LANE), F32)
    _, _, rw_carry, rw_state, dn_carry, dn_state = _mix(
        p_meta.reshape(P_GROUPS, 1, CH, LANE), rw_prm, dn_prm, zc(3 * RW_G + 1),
        jnp.zeros((RW_G, LANE, LANE), F32), zc(3 * DN_HEADS),
        jnp.zeros((DN_HEADS, DN_HEAD, DN_HEAD), F32), 1)

    p = p.reshape(P_GROUPS, batch, seq, LANE)
    ya, yb, _, _, _, _ = _mix(p, rw_prm, dn_prm, rw_carry, rw_state, dn_carry, dn_state,
                              MIX_ROWS if batch % MIX_ROWS == 0 else 1)
    out = _outproj(ya.reshape(RW_G, m, LANE), yb.reshape(DN_HEADS, m, LANE), x2d,
                   w_out[0].astype(BF16), final_norm_w.reshape(1, d), _row_tile(m, OUT_ROW_TILE))
    return out.reshape(batch, seq, d)
```

```python
import functools

import jax
import jax.numpy as jnp
from jax import lax
from jax.experimental import pallas as pl
from jax.experimental.pallas import tpu as pltpu

F32 = jnp.float32
BF16 = jnp.bfloat16

LANE = 128
SUBLANE = 8
CH = 64
N_META = 16
RW_WIDTH = 1024
RW_HEAD = 64
RW_LORA = 64
DN_WIDTH = 1024
DN_HEAD = 128
DN_HEADS = DN_WIDTH // DN_HEAD
CONV_W = 4
NORM_EPS = 1e-6
L2_EPS = 1e-6
RW_GN_EPS = 64e-5
RW_G = RW_WIDTH // LANE
RW_PGROUPS = 4 * RW_G + 1
DN_PGROUPS = 4 * DN_HEADS + 1
P_GROUPS = RW_PGROUPS + DN_PGROUPS
VMEM_LIMIT = 56 * 1024 * 1024
IN_ROW_TILE = 1024
IN_COL_TILE = 768
OUT_ROW_TILE = 512
MIX_ROWS = 4


def _mm(a, b):
    return jnp.dot(a.astype(BF16), b.astype(BF16), preferred_element_type=F32)


def _mm_nt(a, b):
    return lax.dot_general(a.astype(BF16), b.astype(BF16), (((1,), (1,)), ((), ())),
                           preferred_element_type=F32)


def _mm_tn(a, b):
    return lax.dot_general(a.astype(BF16), b.astype(BF16), (((0,), (0,)), ((), ())),
                           preferred_element_type=F32)


def _each(fn, *lists):
    return [fn(*args) for args in zip(*lists)]


def _cumsum_rows(tri, x):
    hi = x.astype(BF16)
    lo = (x - hi.astype(F32)).astype(BF16)
    dot = lambda h: jnp.dot(tri, h, preferred_element_type=F32)
    return dot(hi) + dot(lo)


def _sigmoid(z):
    return 0.5 * jnp.tanh(0.5 * z) + 0.5


def _silu(z):
    h = 0.5 * z
    return h + h * jnp.tanh(h)


def _pair_time_index():
    t = lax.broadcasted_iota(jnp.int32, (CH, 2 * CH), 0)
    lane = lax.broadcasted_iota(jnp.int32, (CH, 2 * CH), 1)
    return t, lane & (CH - 1), lane < CH


def _block_diag(x, first):
    xb = x.astype(BF16)
    zero = jnp.zeros_like(xb)
    return jnp.concatenate([jnp.where(first, xb, zero), jnp.where(first, zero, xb)], axis=0)


def _tri_inverse_each(mats):
    i, j, first = _pair_time_index()
    eye = (i == j).astype(BF16)
    zero = jnp.zeros_like(eye)

    def off(m):
        return ((i & -(2 * m)) == (j & -(2 * m))) & ((i & m) != 0) & ((j & m) == 0)

    neg = [(-a).astype(BF16) for a in mats]
    ts = [jnp.where(off(1), na, eye) for na in neg]
    m = 2
    while m < CH:
        mask = off(m)
        prods = _each(lambda na, t: _mm(jnp.where(mask, na, zero), _block_diag(t, first)), neg, ts)
        corr = _each(lambda t, p: _mm(t, _block_diag(p, first)), ts, prods)
        ts = _each(lambda t, c: jnp.where(mask, c.astype(BF16), t), ts, corr)
        m *= 2
    return ts


def _same_head_mask():
    i2 = lax.broadcasted_iota(jnp.int32, (2 * CH, 2 * CH), 0)
    j2 = lax.broadcasted_iota(jnp.int32, (2 * CH, 2 * CH), 1)
    return (i2 >= CH) == (j2 >= CH)


def _tri_ones():
    ti = lax.broadcasted_iota(jnp.int32, (CH, CH), 0)
    tj = lax.broadcasted_iota(jnp.int32, (CH, CH), 1)
    return (ti >= tj).astype(BF16)


def _inproj_kernel(x_ref, xm_ref, nw_ref, wa_ref, wb_ref, o_ref, om_ref, u_ref, um_ref,
                   *, n_sub, n_main):
    i = pl.program_id(0)
    j = pl.program_id(1)

    def normed(x):
        ms = jnp.mean(x * x, axis=-1, keepdims=True)
        return (x * lax.rsqrt(ms + NORM_EPS) * nw_ref[...]).astype(BF16)

    @pl.when(j == 0)
    def _():
        u_ref[...] = normed(x_ref[...])

    @pl.when((j == 0) & (i == 0))
    def _():
        um_ref[...] = normed(xm_ref[...])

    def project(u, w_t, out_ref):
        acc = lax.dot_general(u, w_t, (((1,), (1,)), ((), ())), preferred_element_type=F32)
        for s in range(n_sub):
            out_ref[s] = acc[:, s * LANE:(s + 1) * LANE]

    def step(w_ref):
        project(u_ref[...], w_ref[...], o_ref)

        @pl.when(i == 0)
        def _():
            project(um_ref[...], w_ref[...], om_ref)

    @pl.when(j < n_main)
    def _():
        step(wa_ref)

    @pl.when(j >= n_main)
    def _():
        step(wb_ref)


def _inproj(x2d, x_meta, norm_w, w_main, w_tail, tm, tn):
    m, d = x2d.shape
    n_tail = w_tail.shape[0] // tn
    n_col = P_GROUPS * LANE // tn
    n_main = n_col - n_tail
    n_sub = tn // LANE
    return pl.pallas_call(
        functools.partial(_inproj_kernel, n_sub=n_sub, n_main=n_main),
        grid=(m // tm, n_col),
        in_specs=[
            pl.BlockSpec((tm, d), lambda i, j: (i, 0)),
            pl.BlockSpec((CH, d), lambda i, j: (0, 0)),
            pl.BlockSpec((1, d), lambda i, j: (0, 0)),
            pl.BlockSpec((tn, d), lambda i, j: (jnp.minimum(j, n_main - 1), 0)),
            pl.BlockSpec((tn, d), lambda i, j: (jnp.maximum(j - n_main, 0), 0)),
        ],
        out_specs=[
            pl.BlockSpec((n_sub, tm, LANE), lambda i, j: (j, i, 0)),
            pl.BlockSpec((n_sub, CH, LANE), lambda i, j: (jnp.where(i == 0, j, n_col - 1), 0, 0)),
        ],
        out_shape=[jax.ShapeDtypeStruct((P_GROUPS, m, LANE), F32),
                   jax.ShapeDtypeStruct((P_GROUPS, CH, LANE), F32)],
        scratch_shapes=[pltpu.VMEM((tm, d), BF16), pltpu.VMEM((CH, d), BF16)],
        compiler_params=pltpu.CompilerParams(
            dimension_semantics=("arbitrary", "arbitrary"), vmem_limit_bytes=VMEM_LIMIT),
        name="inproj",
    )(x2d, x_meta, norm_w, w_main, w_tail)


def _rwkv_rows(p_ref, mu_ref, w0_ref, a0_ref, kk_ref, ka_ref, rk_ref, gnw_ref, gnb_ref,
               w2a2_ref, y_ref, s_ref, c_ref, *, row_ids):
    t_i, s_i, lo = _pair_time_index()
    strict = t_i > s_i
    incl = t_i >= s_i
    same_head = _same_head_mask()
    tri = _tri_ones()
    items = [(ri, g) for ri in row_ids for g in range(RW_G)]
    bd = lambda x: _block_diag(x, lo)

    def seg_sum(x):
        s_lo = jnp.sum(jnp.where(lo, x, 0.0), axis=-1, keepdims=True)
        s_hi = jnp.sum(jnp.where(lo, 0.0, x), axis=-1, keepdims=True)
        return jnp.where(lo, s_lo, s_hi)

    def shifted(group, ri, mu_idx, tail_idx):
        x = p_ref[group, ri]
        c_ref[ri, tail_idx, SUBLANE:, :] = x
        prev = c_ref[ri, tail_idx, SUBLANE - 1:SUBLANE - 1 + CH, :]
        return x + (prev - x) * mu_ref[mu_idx]

    loras = {}
    for ri in row_ids:
        lg = shifted(3 * RW_G, ri, 3 * RW_G, 3 * RW_G)
        xl = jnp.where(lo, jnp.tanh(lg), lg)
        loras[ri] = jnp.dot(xl.astype(BF16), w2a2_ref[...], preferred_element_type=F32)

    def mix_inputs(item):
        ri, g = item
        r = shifted(g, ri, g, g)
        k = shifted(RW_G + g, ri, RW_G + g, RW_G + g)
        v = shifted(2 * RW_G + g, ri, 2 * RW_G + g, 2 * RW_G + g)
        lw = -jnp.exp(-0.5) * _sigmoid(w0_ref[g] + loras[ri][:, g * LANE:(g + 1) * LANE])
        a = _sigmoid(a0_ref[g] + loras[ri][:, RW_WIDTH + g * LANE:RW_WIDTH + (g + 1) * LANE])
        kx = k * kk_ref[g]
        kk = kx * lax.rsqrt(seg_sum(kx * kx) + L2_EPS)
        return r, v, lw, kk, k * (1.0 + (a - 1.0) * ka_ref[g]), kk * a

    r_s, v_s, logw, kk_s, kmod, bb = zip(*[mix_inputs(item) for item in items])

    gam = []
    for n in range(len(row_ids)):
        wide = _cumsum_rows(tri, jnp.concatenate(logw[n * RW_G:(n + 1) * RW_G], axis=1))
        gam += [wide[:, g * LANE:(g + 1) * LANE] for g in range(RW_G)]

    def decay_scaled(x, lw, r, kk, k, b_):
        e_in = jnp.exp(x)
        e_inv = jnp.exp(-x)
        return (e_in, jnp.exp(x[CH - 1:CH, :] - x), r * e_in, kk * jnp.exp(x - lw),
                b_ * e_inv, k * e_inv)

    e_in, e_rem, rt, kt, b_inv, k_inv = zip(*_each(decay_scaled, gam, logw, r_s, kk_s, kmod, bb))

    a_all = _each(
        lambda kt_, rt_, b_, k_: _mm_nt(jnp.concatenate([kt_, rt_], axis=0),
                                        jnp.concatenate([bd(b_), bd(k_)], axis=0)),
        kt, rt, b_inv, k_inv)
    a_kb = [jnp.where(strict, a[:CH, :LANE], 0.0) for a in a_all]
    a_kk = [jnp.where(strict, a[:CH, LANE:], 0.0) for a in a_all]
    a_rb = [jnp.where(incl, a[CH:, :LANE], 0.0) for a in a_all]
    a_rk = [jnp.where(incl, a[CH:, LANE:], 0.0) for a in a_all]
    t_inv = yield a_kb

    av = _each(lambda kk_, rk_, v: _mm(jnp.concatenate([kk_, rk_], axis=0), bd(v)),
               a_kk, a_rk, v_s)
    yield
    tx = _each(lambda t, kt_, av_: _mm(t, jnp.concatenate([bd(kt_), bd(av_[:CH])], axis=1)),
               t_inv, kt, av)
    w_nat = [x[:, :LANE] for x in tx]
    u0 = [x[:, LANE:] for x in tx]
    y0 = [x[CH:] for x in av]

    yield
    s_old = [s_ref[ri, g] for ri, g in items]
    wh = _each(lambda w, r, s: _mm_nt(jnp.concatenate([w, r], axis=0), s), w_nat, rt, s_old)
    yield
    u = _each(lambda wh_, u0_: wh_[:CH] + u0_, wh, u0)
    arbu = _each(lambda a, u_: _mm(a, bd(u_)), a_rb, u)
    yield
    y = _each(lambda wh_, y0_, x: wh_[CH:] + y0_ - x, wh, y0, arbu)
    s_add = _each(
        lambda v, u_, k, b, e: _mm_tn(jnp.concatenate([v, u_], axis=0),
                                      jnp.concatenate([k * e, -(b * e)], axis=0)),
        v_s, u, kmod, bb, e_rem)
    for (ri, g), s, e, add in zip(items, s_old, e_in, s_add):
        s_ref[ri, g] = s * e[CH - 1:CH, :] + jnp.where(same_head, add, 0.0)

    yield
    mean = _each(lambda y_: seg_sum(y_) * (1.0 / RW_HEAD), y)
    dev = _each(lambda y_, m_: y_ - m_, y, mean)
    var = _each(lambda d: seg_sum(d * d) * (1.0 / RW_HEAD), dev)
    bonus = _each(lambda r, k, item: seg_sum(r * k * rk_ref[item[1]]), r_s, kmod, items)

    def finish(d, var_, bonus_, v, item):
        ri, g = item
        yn = d * lax.rsqrt(var_ + RW_GN_EPS) * gnw_ref[g] + gnb_ref[g] + bonus_ * v
        gate = p_ref[3 * RW_G + 1 + g, ri]
        y_ref[g, ri] = (yn * _silu(gate)).astype(y_ref.dtype)

    _each(finish, dev, var, bonus, v_s, items)


def _gdn_rows(p_ref, cw_ref, alog_ref, dtb_ref, nw_ref, y_ref, s_ref, c_ref, *, row_ids):
    t_i, s_i, lo = _pair_time_index()
    strict = t_i > s_i
    incl = t_i >= s_i
    lo_row = lo[:1]
    tri = _tri_ones()
    zero = jnp.zeros((CH, LANE), F32)
    heads = [(ri, h) for ri in row_ids for h in range(DN_HEADS)]
    pairs = [(n, ri, hp) for n, ri in enumerate(row_ids) for hp in range(DN_HEADS // 2)]
    first = lambda xs: xs[0::2]
    second = lambda xs: xs[1::2]

    def conv_silu(group, ri):
        x = p_ref[group, ri]
        c_ref[ri, group, SUBLANE:, :] = x
        acc = x * cw_ref[CONV_W - 1, group]
        for back in range(1, CONV_W):
            acc = acc + (c_ref[ri, group, SUBLANE - back:SUBLANE - back + CH, :]
                         * cw_ref[CONV_W - 1 - back, group])
        return _silu(acc)

    def l2n(x, scale=1.0):
        return x * (lax.rsqrt(jnp.sum(x * x, axis=-1, keepdims=True) + L2_EPS) * scale)

    beta_all, gc_all, gc_t = {}, {}, {}
    for ri in row_ids:
        ba = p_ref[4 * DN_HEADS, ri]
        beta_all[ri] = _sigmoid(ba)
        z = ba + dtb_ref[...]
        softplus = jnp.maximum(z, 0.0) + jnp.log1p(jnp.exp(-jnp.abs(z)))
        gc_all[ri] = _cumsum_rows(tri, -jnp.exp(alog_ref[...]) * softplus)
        gc_t[ri] = jnp.concatenate([gc_all[ri], gc_all[ri]], axis=0).T

    q = [l2n(conv_silu(h, ri), DN_HEAD ** -0.5) for ri, h in heads]
    k = [l2n(conv_silu(DN_HEADS + h, ri)) for ri, h in heads]
    v = [conv_silu(2 * DN_HEADS + h, ri) for ri, h in heads]
    beta = [beta_all[ri][:, h:h + 1] for ri, h in heads]
    gc = [gc_all[ri][:, DN_HEADS + h:DN_HEADS + h + 1] for ri, h in heads]
    kb = _each(lambda x, b: x * b, k, beta)
    vb = _each(lambda x, b: x * b, v, beta)
    e_gc = _each(jnp.exp, gc)
    qe = _each(lambda x, e: x * e, q, e_gc)
    g_last = [x[CH - 1:CH, :] for x in gc]

    def decay_mask(item):
        n, ri, hp = item
        col = jnp.where(lo, gc[n * DN_HEADS + 2 * hp], gc[n * DN_HEADS + 2 * hp + 1])
        r1 = DN_HEADS + 2 * hp
        rowv = jnp.where(lo_row, gc_t[ri][r1:r1 + 1, :], gc_t[ri][r1 + 1:r1 + 2, :])
        return jnp.where(incl, jnp.exp(col - rowv), 0.0)

    decay = _each(decay_mask, pairs)

    def diag2(a, b):
        return jnp.concatenate([jnp.concatenate([a, zero], axis=1),
                                jnp.concatenate([zero, b], axis=1)], axis=0)

    scores = _each(
        lambda kb1, kb2, q1, q2, k1, k2: _mm_nt(
            jnp.concatenate([jnp.concatenate([kb1, kb2], axis=1),
                             jnp.concatenate([q1, q2], axis=1)], axis=0), diag2(k1, k2)),
        first(kb), second(kb), first(q), second(q), first(k), second(k))
    m_mat = _each(lambda s, d: jnp.where(strict, s[:CH] * d, 0.0), scores, decay)
    attn = _each(lambda s, d: jnp.where(incl, s[CH:] * d, 0.0), scores, decay)
    t_inv = yield m_mat

    kbe = _each(lambda x, e: x * e, kb, e_gc)
    uw = _each(
        lambda t, vb1, vb2, kbe1, kbe2: _mm(t, jnp.concatenate([
            jnp.concatenate([vb1, kbe1, zero, zero], axis=1),
            jnp.concatenate([zero, zero, vb2, kbe2], axis=1)], axis=0)),
        t_inv, first(vb), second(vb), first(kbe), second(kbe))

    yield
    s_old = [s_ref[ri, h] for ri, h in heads]
    u_h = [uw[i // 2][:, (i % 2) * 2 * LANE:(i % 2) * 2 * LANE + LANE] for i in range(len(heads))]
    w_h = [uw[i // 2][:, (i % 2) * 2 * LANE + LANE:(i % 2 + 1) * 2 * LANE]
           for i in range(len(heads))]
    ws = _each(lambda w, qe_, s: _mm(jnp.concatenate([w, qe_], axis=0), s), w_h, qe, s_old)
    yield
    v_new = _each(lambda u_, ws_: u_ - ws_[:CH], u_h, ws)
    o2 = _each(lambda a, vn1, vn2: _mm(a, diag2(vn1, vn2)),
               attn, first(v_new), second(v_new))

    yield
    k_dec = _each(lambda x, gl, g: x * jnp.exp(gl - g), k, g_last, gc)
    s_add = _each(_mm_tn, k_dec, v_new)
    yield
    for i, (ri, h) in enumerate(heads):
        s_ref[ri, h] = s_old[i] * jnp.exp(g_last[i]) + s_add[i]
        o = ws[i][CH:] + o2[i // 2][:, (i % 2) * LANE:(i % 2 + 1) * LANE]
        o = o * lax.rsqrt(jnp.mean(o * o, axis=-1, keepdims=True) + NORM_EPS) * nw_ref[...]
        zg = p_ref[3 * DN_HEADS + h, ri]
        y_ref[h, ri] = (o * _silu(zg)).astype(y_ref.dtype)


def _mix_kernel(p_ref, mu_ref, w0_ref, a0_ref, kk_ref, ka_ref, rk_ref, gnw_ref, gnb_ref, w2a2_ref,
                cw_ref, alog_ref, dtb_ref, nw_ref, rw_cin_ref, rw_s0_ref, dn_cin_ref, dn_s0_ref,
                ya_ref, yb_ref, rw_cout_ref, rw_sout_ref, dn_cout_ref, dn_sout_ref,
                rw_s_ref, rw_c_ref, dn_s_ref, dn_c_ref, *, n_chunks, rows):
    c = pl.program_id(1)
    dn_p_ref = p_ref.at[RW_PGROUPS:P_GROUPS]

    @pl.when(c == 0)
    def _():
        for ri in range(rows):
            rw_s_ref[ri] = rw_s0_ref[...]
            rw_c_ref[ri, :, 0:SUBLANE, :] = rw_cin_ref[...]
            dn_s_ref[ri] = dn_s0_ref[...]
            dn_c_ref[ri, :, 0:SUBLANE, :] = dn_cin_ref[...]

    row_ids = tuple(range(rows))
    mixers = [
        _gdn_rows(dn_p_ref, cw_ref, alog_ref, dtb_ref, nw_ref, yb_ref, dn_s_ref, dn_c_ref,
                  row_ids=row_ids),
        _rwkv_rows(p_ref, mu_ref, w0_ref, a0_ref, kk_ref, ka_ref, rk_ref, gnw_ref, gnb_ref,
                   w2a2_ref, ya_ref, rw_s_ref, rw_c_ref, row_ids=row_ids),
    ]
    wanted = [next(mixer) for mixer in mixers]
    inverses = _tri_inverse_each([mat for mats in wanted for mat in mats])
    live = []
    for mixer, mats in zip(mixers, wanted):
        mixer.send(inverses[:len(mats)])
        inverses = inverses[len(mats):]
        live.append(mixer)
    while live:
        for mixer in list(live):
            try:
                next(mixer)
            except StopIteration:
                live.remove(mixer)

    for ri in range(rows):
        rw_c_ref[ri, :, 0:SUBLANE, :] = p_ref[0:3 * RW_G + 1, ri, CH - SUBLANE:CH, :]
        dn_c_ref[ri, :, 0:SUBLANE, :] = dn_p_ref[0:3 * DN_HEADS, ri, CH - SUBLANE:CH, :]

    @pl.when(c == n_chunks - 1)
    def _():
        rw_cout_ref[...] = rw_c_ref[rows - 1, :, 0:SUBLANE, :]
        rw_sout_ref[...] = rw_s_ref[rows - 1]
        dn_cout_ref[...] = dn_c_ref[rows - 1, :, 0:SUBLANE, :]
        dn_sout_ref[...] = dn_s_ref[rows - 1]


def _mix(p, rw_prm, dn_prm, rw_carry, rw_state, dn_carry, dn_state, rows):
    _, batch, seq, _ = p.shape
    n_chunks = seq // CH
    full = lambda shape: pl.BlockSpec(shape, lambda b, c: (0,) * len(shape))
    rw_carry_n = 3 * RW_G + 1
    dn_carry_n = 3 * DN_HEADS
    seq_block = lambda groups: pl.BlockSpec((groups, rows, CH, LANE), lambda b, c: (0, b, c, 0))
    rw_group = full((RW_G, 1, LANE))
    return pl.pallas_call(
        functools.partial(_mix_kernel, n_chunks=n_chunks, rows=rows),
        grid=(batch // rows, n_chunks),
        in_specs=[
            seq_block(P_GROUPS),
            full((rw_carry_n, 1, LANE)),
            rw_group, rw_group, rw_group, rw_group, rw_group, rw_group, rw_group,
            full((2 * RW_LORA, 2 * RW_WIDTH)),
            full((CONV_W, dn_carry_n, 1, LANE)),
            full((1, LANE)), full((1, LANE)), full((1, LANE)),
            full((rw_carry_n, SUBLANE, LANE)), full((RW_G, LANE, LANE)),
            full((dn_carry_n, SUBLANE, LANE)), full((DN_HEADS, DN_HEAD, DN_HEAD)),
        ],
        out_specs=[
            seq_block(RW_G), seq_block(DN_HEADS),
            full((rw_carry_n, SUBLANE, LANE)), full((RW_G, LANE, LANE)),
            full((dn_carry_n, SUBLANE, LANE)), full((DN_HEADS, DN_HEAD, DN_HEAD)),
        ],
        out_shape=[
            jax.ShapeDtypeStruct((RW_G, batch, seq, LANE), BF16),
            jax.ShapeDtypeStruct((DN_HEADS, batch, seq, LANE), BF16),
            jax.ShapeDtypeStruct((rw_carry_n, SUBLANE, LANE), F32),
            jax.ShapeDtypeStruct((RW_G, LANE, LANE), F32),
            jax.ShapeDtypeStruct((dn_carry_n, SUBLANE, LANE), F32),
            jax.ShapeDtypeStruct((DN_HEADS, DN_HEAD, DN_HEAD), F32),
        ],
        scratch_shapes=[
            pltpu.VMEM((rows, RW_G, LANE, LANE), F32),
            pltpu.VMEM((rows, rw_carry_n, SUBLANE + CH, LANE), F32),
            pltpu.VMEM((rows, DN_HEADS, DN_HEAD, DN_HEAD), F32),
            pltpu.VMEM((rows, dn_carry_n, SUBLANE + CH, LANE), F32),
        ],
        compiler_params=pltpu.CompilerParams(
            dimension_semantics=("arbitrary", "arbitrary"), vmem_limit_bytes=VMEM_LIMIT),
        name="mixers",
    )(p, rw_prm["mu"], rw_prm["w0"], rw_prm["a0"], rw_prm["k_k"], rw_prm["k_a"], rw_prm["r_k"],
      rw_prm["gn_w"], rw_prm["gn_b"], rw_prm["w2a2"], dn_prm["conv_w"], dn_prm["a_log"],
      dn_prm["dt_bias"], dn_prm["norm_w"], rw_carry, rw_state, dn_carry, dn_state)


def _outproj_kernel(ya_ref, yb_ref, x_ref, w_ref, fnw_ref, o_ref):
    y = jnp.concatenate([ya_ref[g] for g in range(RW_G)] + [yb_ref[h] for h in range(DN_HEADS)],
                        axis=1)
    hid = x_ref[...] + jnp.dot(y, w_ref[...], preferred_element_type=F32)
    ms = jnp.mean(hid * hid, axis=-1, keepdims=True)
    o_ref[...] = hid * lax.rsqrt(ms + NORM_EPS) * fnw_ref[...]


def _outproj(ya, yb, x2d, w_out, fnw, tm):
    m, d = x2d.shape
    return pl.pallas_call(
        _outproj_kernel,
        grid=(m // tm,),
        in_specs=[
            pl.BlockSpec((RW_G, tm, LANE), lambda i: (0, i, 0)),
            pl.BlockSpec((DN_HEADS, tm, LANE), lambda i: (0, i, 0)),
            pl.BlockSpec((tm, d), lambda i: (i, 0)),
            pl.BlockSpec((RW_WIDTH + DN_WIDTH, d), lambda i: (0, 0)),
            pl.BlockSpec((1, d), lambda i: (0, 0)),
        ],
        out_specs=pl.BlockSpec((tm, d), lambda i: (i, 0)),
        out_shape=jax.ShapeDtypeStruct((m, d), F32),
        compiler_params=pltpu.CompilerParams(
            dimension_semantics=("arbitrary",), vmem_limit_bytes=VMEM_LIMIT),
        name="outproj",
    )(ya, yb, x2d, w_out, fnw)


def _row_tile(m, cap):
    t = cap
    while m % t:
        t //= 2
    return t


def kernel(x, meta_tokens, norm_w, w_in, rw_shift_mu, rw_w0, rw_w2, rw_a0, rw_a2, rw_k_k, rw_k_a,
           rw_r_k, rw_gn_w, rw_gn_b, dn_conv_w, dn_A_log, dn_dt_bias, dn_norm_w, w_out,
           final_norm_w):
    batch, seq, d = x.shape
    assert seq % CH == 0 and norm_w.shape[0] == 1
    m = batch * seq

    tn = IN_COL_TILE
    wi = w_in[0].T.astype(BF16)
    c_b = 3 * RW_WIDTH + 2 * RW_LORA + RW_WIDTH + 3 * DN_WIDTH
    c_z = c_b + 2 * DN_HEADS
    c_tail = (P_GROUPS * LANE // tn - 2) * tn
    assert c_tail <= c_b and c_tail % LANE == 0
    w_tail = jnp.concatenate([
        wi[c_tail:c_b], wi[c_z:], wi[c_b:c_z],
        jnp.zeros((LANE - 2 * DN_HEADS, d), wi.dtype)], axis=0)
    grp = lambda t: t.reshape(-1, 1, LANE)
    zeros_l = jnp.zeros((RW_LORA, RW_WIDTH), F32)
    rw_prm = {
        "mu": grp(rw_shift_mu[0]),
        "w0": grp(rw_w0[0]), "a0": grp(rw_a0[0]), "k_k": grp(rw_k_k[0]), "k_a": grp(rw_k_a[0]),
        "r_k": grp(rw_r_k[0]), "gn_w": grp(rw_gn_w[0]), "gn_b": grp(rw_gn_b[0]),
        "w2a2": jnp.concatenate([jnp.concatenate([rw_w2[0], zeros_l], axis=1),
                                 jnp.concatenate([zeros_l, rw_a2[0]], axis=1)], axis=0).astype(BF16),
    }
    lane_vec = lambda t: jnp.zeros((1, LANE), F32).at[0, DN_HEADS:2 * DN_HEADS].set(t)
    dn_prm = {
        "conv_w": dn_conv_w[0].reshape(CONV_W, 3 * DN_HEADS, 1, LANE),
        "a_log": lane_vec(dn_A_log[0]), "dt_bias": lane_vec(dn_dt_bias[0]),
        "norm_w": dn_norm_w[0].reshape(1, LANE),
    }
    nw = norm_w[0].reshape(1, d)

    meta_rows = jnp.concatenate([jnp.zeros((CH - N_META, d), x.dtype), meta_tokens.astype(x.dtype)],
                                axis=0)
    x2d = x.reshape(m, d)
    p, p_meta = _inproj(x2d, meta_rows, nw, wi, w_tail, _row_tile(m, IN_ROW_TILE), tn)
    zc = lambda n: jnp.zeros((n, SUBLANE, LANE), F32)
    _, _, rw_carry, rw_state, dn_carry, dn_state = _mix(
        p_meta.reshape(P_GROUPS, 1, CH, LANE), rw_prm, dn_prm, zc(3 * RW_G + 1),
        jnp.zeros((RW_G, LANE, LANE), F32), zc(3 * DN_HEADS),
        jnp.zeros((DN_HEADS, DN_HEAD, DN_HEAD), F32), 1)

    p = p.reshape(P_GROUPS, batch, seq, LANE)
    ya, yb, _, _, _, _ = _mix(p, rw_prm, dn_prm, rw_carry, rw_state, dn_carry, dn_state,
                              MIX_ROWS if batch % MIX_ROWS == 0 else 1)
    out = _outproj(ya.reshape(RW_G, m, LANE), yb.reshape(DN_HEADS, m, LANE), x2d,
                   w_out[0].astype(BF16), final_norm_w.reshape(1, d), _row_tile(m, OUT_ROW_TILE))
    return out.reshape(batch, seq, d)
```

```python
import functools

import jax
import jax.numpy as jnp
from jax import lax
from jax.experimental import pallas as pl
from jax.experimental.pallas import tpu as pltpu

F32 = jnp.float32
BF16 = jnp.bfloat16

LANE = 128
SUBLANE = 8
CH = 64
N_META = 16
RW_WIDTH = 1024
RW_HEAD = 64
RW_LORA = 64
DN_WIDTH = 1024
DN_HEAD = 128
DN_HEADS = DN_WIDTH // DN_HEAD
CONV_W = 4
NORM_EPS = 1e-6
L2_EPS = 1e-6
RW_GN_EPS = 64e-5
RW_G = RW_WIDTH // LANE
RW_PGROUPS = 4 * RW_G + 1
DN_PGROUPS = 4 * DN_HEADS + 1
P_GROUPS = RW_PGROUPS + DN_PGROUPS
VMEM_LIMIT = 56 * 1024 * 1024
IN_ROW_TILE = 1024
IN_COL_TILE = 768
OUT_ROW_TILE = 512
MIX_ROWS = 4


def _mm(a, b):
    return jnp.dot(a.astype(BF16), b.astype(BF16), preferred_element_type=F32)


def _mm_nt(a, b):
    return lax.dot_general(a.astype(BF16), b.astype(BF16), (((1,), (1,)), ((), ())),
                           preferred_element_type=F32)


def _mm_tn(a, b):
    return lax.dot_general(a.astype(BF16), b.astype(BF16), (((0,), (0,)), ((), ())),
                           preferred_element_type=F32)


def _each(fn, *lists):
    return [fn(*args) for args in zip(*lists)]


def _cumsum_rows(tri, x):
    hi = x.astype(BF16)
    lo = (x - hi.astype(F32)).astype(BF16)
    dot = lambda h: jnp.dot(tri, h, preferred_element_type=F32)
    return dot(hi) + dot(lo)


def _sigmoid(z):
    return 0.5 * jnp.tanh(0.5 * z) + 0.5


def _silu(z):
    h = 0.5 * z
    return h + h * jnp.tanh(h)


def _pair_time_index():
    t = lax.broadcasted_iota(jnp.int32, (CH, 2 * CH), 0)
    lane = lax.broadcasted_iota(jnp.int32, (CH, 2 * CH), 1)
    return t, lane & (CH - 1), lane < CH


def _block_diag(x, first):
    xb = x.astype(BF16)
    zero = jnp.zeros_like(xb)
    return jnp.concatenate([jnp.where(first, xb, zero), jnp.where(first, zero, xb)], axis=0)


def _tri_inverse_each(mats):
    i, j, first = _pair_time_index()
    eye = (i == j).astype(BF16)
    zero = jnp.zeros_like(eye)

    def off(m):
        return ((i & -(2 * m)) == (j & -(2 * m))) & ((i & m) != 0) & ((j & m) == 0)

    neg = [(-a).astype(BF16) for a in mats]
    ts = [jnp.where(off(1), na, eye) for na in neg]
    m = 2
    while m < CH:
        mask = off(m)
        prods = _each(lambda na, t: _mm(jnp.where(mask, na, zero), _block_diag(t, first)), neg, ts)
        corr = _each(lambda t, p: _mm(t, _block_diag(p, first)), ts, prods)
        ts = _each(lambda t, c: jnp.where(mask, c.astype(BF16), t), ts, corr)
        m *= 2
    return ts


def _same_head_mask():
    i2 = lax.broadcasted_iota(jnp.int32, (2 * CH, 2 * CH), 0)
    j2 = lax.broadcasted_iota(jnp.int32, (2 * CH, 2 * CH), 1)
    return (i2 >= CH) == (j2 >= CH)


def _tri_ones():
    ti = lax.broadcasted_iota(jnp.int32, (CH, CH), 0)
    tj = lax.broadcasted_iota(jnp.int32, (CH, CH), 1)
    return (ti >= tj).astype(BF16)


def _inproj_kernel(x_ref, xm_ref, nw_ref, wa_ref, wb_ref, o_ref, om_ref, u_ref, um_ref,
                   *, n_sub, n_main):
    i = pl.program_id(0)
    j = pl.program_id(1)

    def normed(x):
        ms = jnp.mean(x * x, axis=-1, keepdims=True)
        return (x * lax.rsqrt(ms + NORM_EPS) * nw_ref[...]).astype(BF16)

    @pl.when(j == 0)
    def _():
        u_ref[...] = normed(x_ref[...])

    @pl.when((j == 0) & (i == 0))
    def _():
        um_ref[...] = normed(xm_ref[...])

    def project(u, w_t, out_ref):
        acc = lax.dot_general(u, w_t, (((1,), (1,)), ((), ())), preferred_element_type=F32)
        for s in range(n_sub):
            out_ref[s] = acc[:, s * LANE:(s + 1) * LANE].astype(out_ref.dtype)

    def step(w_ref):
        project(u_ref[...], w_ref[...], o_ref)

        @pl.when(i == 0)
        def _():
            project(um_ref[...], w_ref[...], om_ref)

    @pl.when(j < n_main)
    def _():
        step(wa_ref)

    @pl.when(j >= n_main)
    def _():
        step(wb_ref)


def _inproj(x2d, x_meta, norm_w, w_main, w_tail, tm, tn):
    m, d = x2d.shape
    n_tail = w_tail.shape[0] // tn
    n_col = P_GROUPS * LANE // tn
    n_main = n_col - n_tail
    n_sub = tn // LANE
    return pl.pallas_call(
        functools.partial(_inproj_kernel, n_sub=n_sub, n_main=n_main),
        grid=(m // tm, n_col),
        in_specs=[
            pl.BlockSpec((tm, d), lambda i, j: (i, 0)),
            pl.BlockSpec((CH, d), lambda i, j: (0, 0)),
            pl.BlockSpec((1, d), lambda i, j: (0, 0)),
            pl.BlockSpec((tn, d), lambda i, j: (jnp.minimum(j, n_main - 1), 0)),
            pl.BlockSpec((tn, d), lambda i, j: (jnp.maximum(j - n_main, 0), 0)),
        ],
        out_specs=[
            pl.BlockSpec((n_sub, tm, LANE), lambda i, j: (j, i, 0)),
            pl.BlockSpec((n_sub, CH, LANE), lambda i, j: (jnp.where(i == 0, j, n_col - 1), 0, 0)),
        ],
        out_shape=[jax.ShapeDtypeStruct((P_GROUPS, m, LANE), BF16),
                   jax.ShapeDtypeStruct((P_GROUPS, CH, LANE), BF16)],
        scratch_shapes=[pltpu.VMEM((tm, d), BF16), pltpu.VMEM((CH, d), BF16)],
        compiler_params=pltpu.CompilerParams(
            dimension_semantics=("arbitrary", "arbitrary"), vmem_limit_bytes=VMEM_LIMIT),
        name="inproj",
    )(x2d, x_meta, norm_w, w_main, w_tail)


def _rwkv_rows(p_ref, mu_ref, w0_ref, a0_ref, kk_ref, ka_ref, rk_ref, gnw_ref, gnb_ref,
               w2a2_ref, y_ref, s_ref, c_ref, *, row_ids):
    t_i, s_i, lo = _pair_time_index()
    strict = t_i > s_i
    incl = t_i >= s_i
    same_head = _same_head_mask()
    tri = _tri_ones()
    items = [(ri, g) for ri in row_ids for g in range(RW_G)]
    bd = lambda x: _block_diag(x, lo)

    def seg_sum(x):
        s_lo = jnp.sum(jnp.where(lo, x, 0.0), axis=-1, keepdims=True)
        s_hi = jnp.sum(jnp.where(lo, 0.0, x), axis=-1, keepdims=True)
        return jnp.where(lo, s_lo, s_hi)

    def shifted(group, ri, mu_idx, tail_idx):
        x = p_ref[group, ri].astype(F32)
        c_ref[ri, tail_idx, SUBLANE:, :] = x
        prev = c_ref[ri, tail_idx, SUBLANE - 1:SUBLANE - 1 + CH, :]
        return x + (prev - x) * mu_ref[mu_idx]

    loras = {}
    for ri in row_ids:
        lg = shifted(3 * RW_G, ri, 3 * RW_G, 3 * RW_G)
        xl = jnp.where(lo, jnp.tanh(lg), lg)
        loras[ri] = jnp.dot(xl.astype(BF16), w2a2_ref[...], preferred_element_type=F32)

    def mix_inputs(item):
        ri, g = item
        r = shifted(g, ri, g, g)
        k = shifted(RW_G + g, ri, RW_G + g, RW_G + g)
        v = shifted(2 * RW_G + g, ri, 2 * RW_G + g, 2 * RW_G + g)
        lw = -jnp.exp(-0.5) * _sigmoid(w0_ref[g] + loras[ri][:, g * LANE:(g + 1) * LANE])
        a = _sigmoid(a0_ref[g] + loras[ri][:, RW_WIDTH + g * LANE:RW_WIDTH + (g + 1) * LANE])
        kx = k * kk_ref[g]
        kk = kx * lax.rsqrt(seg_sum(kx * kx) + L2_EPS)
        return r, v, lw, kk, k * (1.0 + (a - 1.0) * ka_ref[g]), kk * a

    r_s, v_s, logw, kk_s, kmod, bb = zip(*[mix_inputs(item) for item in items])

    gam = []
    for n in range(len(row_ids)):
        wide = _cumsum_rows(tri, jnp.concatenate(logw[n * RW_G:(n + 1) * RW_G], axis=1))
        gam += [wide[:, g * LANE:(g + 1) * LANE] for g in range(RW_G)]

    def decay_scaled(x, lw, r, kk, k, b_):
        e_in = jnp.exp(x)
        e_inv = jnp.exp(-x)
        return (e_in, jnp.exp(x[CH - 1:CH, :] - x), r * e_in, kk * jnp.exp(x - lw),
                b_ * e_inv, k * e_inv)

    e_in, e_rem, rt, kt, b_inv, k_inv = zip(*_each(decay_scaled, gam, logw, r_s, kk_s, kmod, bb))

    a_all = _each(
        lambda kt_, rt_, b_, k_: _mm_nt(jnp.concatenate([kt_, rt_], axis=0),
                                        jnp.concatenate([bd(b_), bd(k_)], axis=0)),
        kt, rt, b_inv, k_inv)
    a_kb = [jnp.where(strict, a[:CH, :LANE], 0.0) for a in a_all]
    a_kk = [jnp.where(strict, a[:CH, LANE:], 0.0) for a in a_all]
    a_rb = [jnp.where(incl, a[CH:, :LANE], 0.0) for a in a_all]
    a_rk = [jnp.where(incl, a[CH:, LANE:], 0.0) for a in a_all]
    t_inv = yield a_kb

    av = _each(lambda kk_, rk_, v: _mm(jnp.concatenate([kk_, rk_], axis=0), bd(v)),
               a_kk, a_rk, v_s)
    yield
    tx = _each(lambda t, kt_, av_: _mm(t, jnp.concatenate([bd(kt_), bd(av_[:CH])], axis=1)),
               t_inv, kt, av)
    w_nat = [x[:, :LANE] for x in tx]
    u0 = [x[:, LANE:] for x in tx]
    y0 = [x[CH:] for x in av]

    yield
    s_old = [s_ref[ri, g] for ri, g in items]
    wh = _each(lambda w, r, s: _mm_nt(jnp.concatenate([w, r], axis=0), s), w_nat, rt, s_old)
    yield
    u = _each(lambda wh_, u0_: wh_[:CH] + u0_, wh, u0)
    arbu = _each(lambda a, u_: _mm(a, bd(u_)), a_rb, u)
    yield
    y = _each(lambda wh_, y0_, x: wh_[CH:] + y0_ - x, wh, y0, arbu)
    s_add = _each(
        lambda v, u_, k, b, e: _mm_tn(jnp.concatenate([v, u_], axis=0),
                                      jnp.concatenate([k * e, -(b * e)], axis=0)),
        v_s, u, kmod, bb, e_rem)
    for (ri, g), s, e, add in zip(items, s_old, e_in, s_add):
        s_ref[ri, g] = s * e[CH - 1:CH, :] + jnp.where(same_head, add, 0.0)

    yield
    mean = _each(lambda y_: seg_sum(y_) * (1.0 / RW_HEAD), y)
    dev = _each(lambda y_, m_: y_ - m_, y, mean)
    var = _each(lambda d: seg_sum(d * d) * (1.0 / RW_HEAD), dev)
    bonus = _each(lambda r, k, item: seg_sum(r * k * rk_ref[item[1]]), r_s, kmod, items)

    def finish(d, var_, bonus_, v, item):
        ri, g = item
        yn = d * lax.rsqrt(var_ + RW_GN_EPS) * gnw_ref[g] + gnb_ref[g] + bonus_ * v
        gate = p_ref[3 * RW_G + 1 + g, ri].astype(F32)
        y_ref[g, ri] = (yn * _silu(gate)).astype(y_ref.dtype)

    _each(finish, dev, var, bonus, v_s, items)


def _gdn_rows(p_ref, cw_ref, alog_ref, dtb_ref, nw_ref, y_ref, s_ref, c_ref, *, row_ids):
    t_i, s_i, lo = _pair_time_index()
    strict = t_i > s_i
    incl = t_i >= s_i
    lo_row = lo[:1]
    tri = _tri_ones()
    zero = jnp.zeros((CH, LANE), F32)
    heads = [(ri, h) for ri in row_ids for h in range(DN_HEADS)]
    pairs = [(n, ri, hp) for n, ri in enumerate(row_ids) for hp in range(DN_HEADS // 2)]
    first = lambda xs: xs[0::2]
    second = lambda xs: xs[1::2]

    def conv_silu(group, ri):
        x = p_ref[group, ri].astype(F32)
        c_ref[ri, group, SUBLANE:, :] = x
        acc = x * cw_ref[CONV_W - 1, group]
        for back in range(1, CONV_W):
            acc = acc + (c_ref[ri, group, SUBLANE - back:SUBLANE - back + CH, :]
                         * cw_ref[CONV_W - 1 - back, group])
        return _silu(acc)

    def l2n(x, scale=1.0):
        return x * (lax.rsqrt(jnp.sum(x * x, axis=-1, keepdims=True) + L2_EPS) * scale)

    beta_all, gc_all, gc_t = {}, {}, {}
    for ri in row_ids:
        ba = p_ref[4 * DN_HEADS, ri].astype(F32)
        beta_all[ri] = _sigmoid(ba)
        z = ba + dtb_ref[...]
        softplus = jnp.maximum(z, 0.0) + jnp.log1p(jnp.exp(-jnp.abs(z)))
        gc_all[ri] = _cumsum_rows(tri, -jnp.exp(alog_ref[...]) * softplus)
        gc_t[ri] = jnp.concatenate([gc_all[ri], gc_all[ri]], axis=0).T

    q = [l2n(conv_silu(h, ri), DN_HEAD ** -0.5) for ri, h in heads]
    k = [l2n(conv_silu(DN_HEADS + h, ri)) for ri, h in heads]
    v = [conv_silu(2 * DN_HEADS + h, ri) for ri, h in heads]
    beta = [beta_all[ri][:, h:h + 1] for ri, h in heads]
    gc = [gc_all[ri][:, DN_HEADS + h:DN_HEADS + h + 1] for ri, h in heads]
    kb = _each(lambda x, b: x * b, k, beta)
    vb = _each(lambda x, b: x * b, v, beta)
    e_gc = _each(jnp.exp, gc)
    qe = _each(lambda x, e: x * e, q, e_gc)
    g_last = [x[CH - 1:CH, :] for x in gc]

    def decay_mask(item):
        n, ri, hp = item
        col = jnp.where(lo, gc[n * DN_HEADS + 2 * hp], gc[n * DN_HEADS + 2 * hp + 1])
        r1 = DN_HEADS + 2 * hp
        rowv = jnp.where(lo_row, gc_t[ri][r1:r1 + 1, :], gc_t[ri][r1 + 1:r1 + 2, :])
        return jnp.where(incl, jnp.exp(col - rowv), 0.0)

    decay = _each(decay_mask, pairs)

    def diag2(a, b):
        return jnp.concatenate([jnp.concatenate([a, zero], axis=1),
                                jnp.concatenate([zero, b], axis=1)], axis=0)

    scores = _each(
        lambda kb1, kb2, q1, q2, k1, k2: _mm_nt(
            jnp.concatenate([jnp.concatenate([kb1, kb2], axis=1),
                             jnp.concatenate([q1, q2], axis=1)], axis=0), diag2(k1, k2)),
        first(kb), second(kb), first(q), second(q), first(k), second(k))
    m_mat = _each(lambda s, d: jnp.where(strict, s[:CH] * d, 0.0), scores, decay)
    attn = _each(lambda s, d: jnp.where(incl, s[CH:] * d, 0.0), scores, decay)
    t_inv = yield m_mat

    kbe = _each(lambda x, e: x * e, kb, e_gc)
    uw = _each(
        lambda t, vb1, vb2, kbe1, kbe2: _mm(t, jnp.concatenate([
            jnp.concatenate([vb1, kbe1, zero, zero], axis=1),
            jnp.concatenate([zero, zero, vb2, kbe2], axis=1)], axis=0)),
        t_inv, first(vb), second(vb), first(kbe), second(kbe))

    yield
    s_old = [s_ref[ri, h] for ri, h in heads]
    u_h = [uw[i // 2][:, (i % 2) * 2 * LANE:(i % 2) * 2 * LANE + LANE] for i in range(len(heads))]
    w_h = [uw[i // 2][:, (i % 2) * 2 * LANE + LANE:(i % 2 + 1) * 2 * LANE]
           for i in range(len(heads))]
    ws = _each(lambda w, qe_, s: _mm(jnp.concatenate([w, qe_], axis=0), s), w_h, qe, s_old)
    yield
    v_new = _each(lambda u_, ws_: u_ - ws_[:CH], u_h, ws)
    o2 = _each(lambda a, vn1, vn2: _mm(a, diag2(vn1, vn2)),
               attn, first(v_new), second(v_new))

    yield
    k_dec = _each(lambda x, gl, g: x * jnp.exp(gl - g), k, g_last, gc)
    s_add = _each(_mm_tn, k_dec, v_new)
    yield
    for i, (ri, h) in enumerate(heads):
        s_ref[ri, h] = s_old[i] * jnp.exp(g_last[i]) + s_add[i]
        o = ws[i][CH:] + o2[i // 2][:, (i % 2) * LANE:(i % 2 + 1) * LANE]
        o = o * lax.rsqrt(jnp.mean(o * o, axis=-1, keepdims=True) + NORM_EPS) * nw_ref[...]
        zg = p_ref[3 * DN_HEADS + h, ri].astype(F32)
        y_ref[h, ri] = (o * _silu(zg)).astype(y_ref.dtype)


def _mix_kernel(p_ref, mu_ref, w0_ref, a0_ref, kk_ref, ka_ref, rk_ref, gnw_ref, gnb_ref, w2a2_ref,
                cw_ref, alog_ref, dtb_ref, nw_ref, rw_cin_ref, rw_s0_ref, dn_cin_ref, dn_s0_ref,
                ya_ref, yb_ref, rw_cout_ref, rw_sout_ref, dn_cout_ref, dn_sout_ref,
                rw_s_ref, rw_c_ref, dn_s_ref, dn_c_ref, *, n_chunks, rows):
    c = pl.program_id(1)
    dn_p_ref = p_ref.at[RW_PGROUPS:P_GROUPS]

    @pl.when(c == 0)
    def _():
        for ri in range(rows):
            rw_s_ref[ri] = rw_s0_ref[...]
            rw_c_ref[ri, :, 0:SUBLANE, :] = rw_cin_ref[...]
            dn_s_ref[ri] = dn_s0_ref[...]
            dn_c_ref[ri, :, 0:SUBLANE, :] = dn_cin_ref[...]

    row_ids = tuple(range(rows))
    mixers = [
        _gdn_rows(dn_p_ref, cw_ref, alog_ref, dtb_ref, nw_ref, yb_ref, dn_s_ref, dn_c_ref,
                  row_ids=row_ids),
        _rwkv_rows(p_ref, mu_ref, w0_ref, a0_ref, kk_ref, ka_ref, rk_ref, gnw_ref, gnb_ref,
                   w2a2_ref, ya_ref, rw_s_ref, rw_c_ref, row_ids=row_ids),
    ]
    wanted = [next(mixer) for mixer in mixers]
    inverses = _tri_inverse_each([mat for mats in wanted for mat in mats])
    live = []
    for mixer, mats in zip(mixers, wanted):
        mixer.send(inverses[:len(mats)])
        inverses = inverses[len(mats):]
        live.append(mixer)
    while live:
        for mixer in list(live):
            try:
                next(mixer)
            except StopIteration:
                live.remove(mixer)

    for ri in range(rows):
        rw_c_ref[ri, :, 0:SUBLANE, :] = rw_c_ref[ri, :, CH:CH + SUBLANE, :]
        dn_c_ref[ri, :, 0:SUBLANE, :] = dn_c_ref[ri, :, CH:CH + SUBLANE, :]

    @pl.when(c == n_chunks - 1)
    def _():
        rw_cout_ref[...] = rw_c_ref[rows - 1, :, 0:SUBLANE, :]
        rw_sout_ref[...] = rw_s_ref[rows - 1]
        dn_cout_ref[...] = dn_c_ref[rows - 1, :, 0:SUBLANE, :]
        dn_sout_ref[...] = dn_s_ref[rows - 1]


def _mix(p, rw_prm, dn_prm, rw_carry, rw_state, dn_carry, dn_state, rows):
    _, batch, seq, _ = p.shape
    n_chunks = seq // CH
    full = lambda shape: pl.BlockSpec(shape, lambda b, c: (0,) * len(shape))
    rw_carry_n = 3 * RW_G + 1
    dn_carry_n = 3 * DN_HEADS
    seq_block = lambda groups: pl.BlockSpec((groups, rows, CH, LANE), lambda b, c: (0, b, c, 0))
    rw_group = full((RW_G, 1, LANE))
    return pl.pallas_call(
        functools.partial(_mix_kernel, n_chunks=n_chunks, rows=rows),
        grid=(batch // rows, n_chunks),
        in_specs=[
            seq_block(P_GROUPS),
            full((rw_carry_n, 1, LANE)),
            rw_group, rw_group, rw_group, rw_group, rw_group, rw_group, rw_group,
            full((2 * RW_LORA, 2 * RW_WIDTH)),
            full((CONV_W, dn_carry_n, 1, LANE)),
            full((1, LANE)), full((1, LANE)), full((1, LANE)),
            full((rw_carry_n, SUBLANE, LANE)), full((RW_G, LANE, LANE)),
            full((dn_carry_n, SUBLANE, LANE)), full((DN_HEADS, DN_HEAD, DN_HEAD)),
        ],
        out_specs=[
            seq_block(RW_G), seq_block(DN_HEADS),
            full((rw_carry_n, SUBLANE, LANE)), full((RW_G, LANE, LANE)),
            full((dn_carry_n, SUBLANE, LANE)), full((DN_HEADS, DN_HEAD, DN_HEAD)),
        ],
        out_shape=[
            jax.ShapeDtypeStruct((RW_G, batch, seq, LANE), BF16),
            jax.ShapeDtypeStruct((DN_HEADS, batch, seq, LANE), BF16),
            jax.ShapeDtypeStruct((rw_carry_n, SUBLANE, LANE), F32),
            jax.ShapeDtypeStruct((RW_G, LANE, LANE), F32),
            jax.ShapeDtypeStruct((dn_carry_n, SUBLANE, LANE), F32),
            jax.ShapeDtypeStruct((DN_HEADS, DN_HEAD, DN_HEAD), F32),
        ],
        scratch_shapes=[
            pltpu.VMEM((rows, RW_G, LANE, LANE), F32),
            pltpu.VMEM((rows, rw_carry_n, SUBLANE + CH, LANE), F32),
            pltpu.VMEM((rows, DN_HEADS, DN_HEAD, DN_HEAD), F32),
            pltpu.VMEM((rows, dn_carry_n, SUBLANE + CH, LANE), F32),
        ],
        compiler_params=pltpu.CompilerParams(
            dimension_semantics=("arbitrary", "arbitrary"), vmem_limit_bytes=VMEM_LIMIT),
        name="mixers",
    )(p, rw_prm["mu"], rw_prm["w0"], rw_prm["a0"], rw_prm["k_k"], rw_prm["k_a"], rw_prm["r_k"],
      rw_prm["gn_w"], rw_prm["gn_b"], rw_prm["w2a2"], dn_prm["conv_w"], dn_prm["a_log"],
      dn_prm["dt_bias"], dn_prm["norm_w"], rw_carry, rw_state, dn_carry, dn_state)


def _outproj_kernel(ya_ref, yb_ref, x_ref, w_ref, fnw_ref, o_ref):
    y = jnp.concatenate([ya_ref[g] for g in range(RW_G)] + [yb_ref[h] for h in range(DN_HEADS)],
                        axis=1)
    hid = x_ref[...] + jnp.dot(y, w_ref[...], preferred_element_type=F32)
    ms = jnp.mean(hid * hid, axis=-1, keepdims=True)
    o_ref[...] = hid * lax.rsqrt(ms + NORM_EPS) * fnw_ref[...]


def _outproj(ya, yb, x2d, w_out, fnw, tm):
    m, d = x2d.shape
    return pl.pallas_call(
        _outproj_kernel,
        grid=(m // tm,),
        in_specs=[
            pl.BlockSpec((RW_G, tm, LANE), lambda i: (0, i, 0)),
            pl.BlockSpec((DN_HEADS, tm, LANE), lambda i: (0, i, 0)),
            pl.BlockSpec((tm, d), lambda i: (i, 0)),
            pl.BlockSpec((RW_WIDTH + DN_WIDTH, d), lambda i: (0, 0)),
            pl.BlockSpec((1, d), lambda i: (0, 0)),
        ],
        out_specs=pl.BlockSpec((tm, d), lambda i: (i, 0)),
        out_shape=jax.ShapeDtypeStruct((m, d), F32),
        compiler_params=pltpu.CompilerParams(
            dimension_semantics=("arbitrary",), vmem_limit_bytes=VMEM_LIMIT),
        name="outproj",
    )(ya, yb, x2d, w_out, fnw)


def _row_tile(m, cap):
    t = cap
    while m % t:
        t //= 2
    return t


def kernel(x, meta_tokens, norm_w, w_in, rw_shift_mu, rw_w0, rw_w2, rw_a0, rw_a2, rw_k_k, rw_k_a,
           rw_r_k, rw_gn_w, rw_gn_b, dn_conv_w, dn_A_log, dn_dt_bias, dn_norm_w, w_out,
           final_norm_w):
    batch, seq, d = x.shape
    assert seq % CH == 0 and norm_w.shape[0] == 1
    m = batch * seq

    tn = IN_COL_TILE
    wi = w_in[0].T.astype(BF16)
    c_b = 3 * RW_WIDTH + 2 * RW_LORA + RW_WIDTH + 3 * DN_WIDTH
    c_z = c_b + 2 * DN_HEADS
    c_tail = (P_GROUPS * LANE // tn - 2) * tn
    assert c_tail <= c_b and c_tail % LANE == 0
    w_tail = jnp.concatenate([
        wi[c_tail:c_b], wi[c_z:], wi[c_b:c_z],
        jnp.zeros((LANE - 2 * DN_HEADS, d), wi.dtype)], axis=0)
    grp = lambda t: t.reshape(-1, 1, LANE)
    zeros_l = jnp.zeros((RW_LORA, RW_WIDTH), F32)
    rw_prm = {
        "mu": grp(rw_shift_mu[0]),
        "w0": grp(rw_w0[0]), "a0": grp(rw_a0[0]), "k_k": grp(rw_k_k[0]), "k_a": grp(rw_k_a[0]),
        "r_k": grp(rw_r_k[0]), "gn_w": grp(rw_gn_w[0]), "gn_b": grp(rw_gn_b[0]),
        "w2a2": jnp.concatenate([jnp.concatenate([rw_w2[0], zeros_l], axis=1),
                                 jnp.concatenate([zeros_l, rw_a2[0]], axis=1)], axis=0).astype(BF16),
    }
    lane_vec = lambda t: jnp.zeros((1, LANE), F32).at[0, DN_HEADS:2 * DN_HEADS].set(t)
    dn_prm = {
        "conv_w": dn_conv_w[0].reshape(CONV_W, 3 * DN_HEADS, 1, LANE),
        "a_log": lane_vec(dn_A_log[0]), "dt_bias": lane_vec(dn_dt_bias[0]),
        "norm_w": dn_norm_w[0].reshape(1, LANE),
    }
    nw = norm_w[0].reshape(1, d)

    meta_rows = jnp.concatenate([jnp.zeros((CH - N_META, d), x.dtype), meta_tokens.astype(x.dtype)],
                                axis=0)
    x2d = x.reshape(m, d)
    p, p_meta = _inproj(x2d, meta_rows, nw, wi, w_tail, _row_tile(m, IN_ROW_TILE), tn)
    zc = lambda n: jnp.zeros((n, SUBLANE, LANE), F32)
    _, _, rw_carry, rw_state, dn_carry, dn_state = _mix(
        p_meta.reshape(P_GROUPS, 1, CH, LANE), rw_prm, dn_prm, zc(3 * RW_G + 1),
        jnp.zeros((RW_G, LANE, LANE), F32), zc(3 * DN_HEADS),
        jnp.zeros((DN_HEADS, DN_HEAD, DN_HEAD), F32), 1)

    p = p.reshape(P_GROUPS, batch, seq, LANE)
    ya, yb, _, _, _, _ = _mix(p, rw_prm, dn_prm, rw_carry, rw_state, dn_carry, dn_state,
                              MIX_ROWS if batch % MIX_ROWS == 0 else 1)
    out = _outproj(ya.reshape(RW_G, m, LANE), yb.reshape(DN_HEADS, m, LANE), x2d,
                   w_out[0].astype(BF16), final_norm_w.reshape(1, d), _row_tile(m, OUT_ROW_TILE))
    return out.reshape(batch, seq, d)
```

```python
import functools

import jax
import jax.numpy as jnp
from jax import lax
from jax.experimental import pallas as pl
from jax.experimental.pallas import tpu as pltpu

F32 = jnp.float32
BF16 = jnp.bfloat16

LANE = 128
SUBLANE = 8
CH = 64
N_META = 16
RW_WIDTH = 1024
RW_HEAD = 64
RW_LORA = 64
DN_WIDTH = 1024
DN_HEAD = 128
DN_HEADS = DN_WIDTH // DN_HEAD
CONV_W = 4
NORM_EPS = 1e-6
L2_EPS = 1e-6
RW_GN_EPS = 64e-5
RW_G = RW_WIDTH // LANE
RW_PGROUPS = 4 * RW_G + 1
DN_PGROUPS = 4 * DN_HEADS + 1
P_GROUPS = RW_PGROUPS + DN_PGROUPS
VMEM_LIMIT = 56 * 1024 * 1024
IN_ROW_TILE = 1024
IN_COL_TILE = 768
OUT_ROW_TILE = 512
MIX_ROWS = 4


def _mm(a, b):
    return jnp.dot(a.astype(BF16), b.astype(BF16), preferred_element_type=F32)


def _mm_nt(a, b):
    return lax.dot_general(a.astype(BF16), b.astype(BF16), (((1,), (1,)), ((), ())),
                           preferred_element_type=F32)


def _mm_tn(a, b):
    return lax.dot_general(a.astype(BF16), b.astype(BF16), (((0,), (0,)), ((), ())),
                           preferred_element_type=F32)


def _each(fn, *lists):
    return [fn(*args) for args in zip(*lists)]


def _cumsum_rows(tri, x):
    hi = x.astype(BF16)
    lo = (x - hi.astype(F32)).astype(BF16)
    dot = lambda h: jnp.dot(tri, h, preferred_element_type=F32)
    return dot(hi) + dot(lo)


def _sigmoid(z):
    return 0.5 * jnp.tanh(0.5 * z) + 0.5


def _silu(z):
    h = 0.5 * z
    return h + h * jnp.tanh(h)


def _pair_time_index():
    t = lax.broadcasted_iota(jnp.int32, (CH, 2 * CH), 0)
    lane = lax.broadcasted_iota(jnp.int32, (CH, 2 * CH), 1)
    return t, lane & (CH - 1), lane < CH


def _block_diag(x, first):
    xb = x.astype(BF16)
    zero = jnp.zeros_like(xb)
    return jnp.concatenate([jnp.where(first, xb, zero), jnp.where(first, zero, xb)], axis=0)


def _tri_inverse_each(mats):
    i, j, first = _pair_time_index()
    eye = (i == j).astype(BF16)
    zero = jnp.zeros_like(eye)

    def off(m):
        return ((i & -(2 * m)) == (j & -(2 * m))) & ((i & m) != 0) & ((j & m) == 0)

    neg = [(-a).astype(BF16) for a in mats]
    ts = [jnp.where(off(1), na, eye) for na in neg]
    m = 2
    while m < CH:
        mask = off(m)
        prods = _each(lambda na, t: _mm(jnp.where(mask, na, zero), _block_diag(t, first)), neg, ts)
        corr = _each(lambda t, p: _mm(t, _block_diag(p, first)), ts, prods)
        ts = _each(lambda t, c: jnp.where(mask, c.astype(BF16), t), ts, corr)
        m *= 2
    return ts


def _same_head_mask():
    i2 = lax.broadcasted_iota(jnp.int32, (2 * CH, 2 * CH), 0)
    j2 = lax.broadcasted_iota(jnp.int32, (2 * CH, 2 * CH), 1)
    return (i2 >= CH) == (j2 >= CH)


def _tri_ones():
    ti = lax.broadcasted_iota(jnp.int32, (CH, CH), 0)
    tj = lax.broadcasted_iota(jnp.int32, (CH, CH), 1)
    return (ti >= tj).astype(BF16)


def _inproj_kernel(x_ref, xm_ref, nw_ref, wa_ref, wb_ref, o_ref, om_ref, u_ref, um_ref,
                   *, n_sub, n_main):
    i = pl.program_id(0)
    j = pl.program_id(1)

    def normed(x):
        ms = jnp.mean(x * x, axis=-1, keepdims=True)
        return (x * lax.rsqrt(ms + NORM_EPS) * nw_ref[...]).astype(BF16)

    @pl.when(j == 0)
    def _():
        u_ref[...] = normed(x_ref[...])

    @pl.when((j == 0) & (i == 0))
    def _():
        um_ref[...] = normed(xm_ref[...])

    def project(u, w_t, out_ref):
        acc = lax.dot_general(u, w_t, (((1,), (1,)), ((), ())), preferred_element_type=F32)
        for s in range(n_sub):
            out_ref[s] = acc[:, s * LANE:(s + 1) * LANE]

    def step(w_ref):
        project(u_ref[...], w_ref[...], o_ref)

        @pl.when(i == 0)
        def _():
            project(um_ref[...], w_ref[...], om_ref)

    @pl.when(j < n_main)
    def _():
        step(wa_ref)

    @pl.when(j >= n_main)
    def _():
        step(wb_ref)


def _inproj(x2d, x_meta, norm_w, w_main, w_tail, tm, tn):
    m, d = x2d.shape
    n_tail = w_tail.shape[0] // tn
    n_col = P_GROUPS * LANE // tn
    n_main = n_col - n_tail
    n_sub = tn // LANE
    return pl.pallas_call(
        functools.partial(_inproj_kernel, n_sub=n_sub, n_main=n_main),
        grid=(m // tm, n_col),
        in_specs=[
            pl.BlockSpec((tm, d), lambda i, j: (i, 0)),
            pl.BlockSpec((CH, d), lambda i, j: (0, 0)),
            pl.BlockSpec((1, d), lambda i, j: (0, 0)),
            pl.BlockSpec((tn, d), lambda i, j: (jnp.minimum(j, n_main - 1), 0)),
            pl.BlockSpec((tn, d), lambda i, j: (jnp.maximum(j - n_main, 0), 0)),
        ],
        out_specs=[
            pl.BlockSpec((n_sub, tm, LANE), lambda i, j: (j, i, 0)),
            pl.BlockSpec((n_sub, CH, LANE), lambda i, j: (jnp.where(i == 0, j, n_col - 1), 0, 0)),
        ],
        out_shape=[jax.ShapeDtypeStruct((P_GROUPS, m, LANE), F32),
                   jax.ShapeDtypeStruct((P_GROUPS, CH, LANE), F32)],
        scratch_shapes=[pltpu.VMEM((tm, d), BF16), pltpu.VMEM((CH, d), BF16)],
        compiler_params=pltpu.CompilerParams(
            dimension_semantics=("arbitrary", "arbitrary"), vmem_limit_bytes=VMEM_LIMIT),
        name="inproj",
    )(x2d, x_meta, norm_w, w_main, w_tail)


def _rwkv_rows(p_ref, mu_ref, w0_ref, a0_ref, kk_ref, ka_ref, rk_ref, gnw_ref, gnb_ref,
               w2a2_ref, y_ref, s_ref, c_ref, *, row_ids):
    t_i, s_i, lo = _pair_time_index()
    strict = t_i > s_i
    incl = t_i >= s_i
    same_head = _same_head_mask()
    tri = _tri_ones()
    items = [(ri, g) for ri in row_ids for g in range(RW_G)]
    bd = lambda x: _block_diag(x, lo)

    def seg_sum(x):
        s_lo = jnp.sum(jnp.where(lo, x, 0.0), axis=-1, keepdims=True)
        s_hi = jnp.sum(jnp.where(lo, 0.0, x), axis=-1, keepdims=True)
        return jnp.where(lo, s_lo, s_hi)

    def shifted(group, ri, mu_idx, tail_idx):
        x = p_ref[group, ri]
        c_ref[ri, tail_idx, SUBLANE:, :] = x
        prev = c_ref[ri, tail_idx, SUBLANE - 1:SUBLANE - 1 + CH, :]
        return x + (prev - x) * mu_ref[mu_idx]

    loras = {}
    for ri in row_ids:
        lg = shifted(3 * RW_G, ri, 3 * RW_G, 3 * RW_G)
        xl = jnp.where(lo, jnp.tanh(lg), lg)
        loras[ri] = jnp.dot(xl.astype(BF16), w2a2_ref[...], preferred_element_type=F32)

    def mix_inputs(item):
        ri, g = item
        r = shifted(g, ri, g, g)
        k = shifted(RW_G + g, ri, RW_G + g, RW_G + g)
        v = shifted(2 * RW_G + g, ri, 2 * RW_G + g, 2 * RW_G + g)
        lw = -jnp.exp(-0.5) * _sigmoid(w0_ref[g] + loras[ri][:, g * LANE:(g + 1) * LANE])
        a = _sigmoid(a0_ref[g] + loras[ri][:, RW_WIDTH + g * LANE:RW_WIDTH + (g + 1) * LANE])
        kx = k * kk_ref[g]
        kk = kx * lax.rsqrt(seg_sum(kx * kx) + L2_EPS)
        return r, v, lw, kk, k * (1.0 + (a - 1.0) * ka_ref[g]), kk * a

    r_s, v_s, logw, kk_s, kmod, bb = zip(*[mix_inputs(item) for item in items])

    gam = []
    for n in range(len(row_ids)):
        wide = _cumsum_rows(tri, jnp.concatenate(logw[n * RW_G:(n + 1) * RW_G], axis=1))
        gam += [wide[:, g * LANE:(g + 1) * LANE] for g in range(RW_G)]

    def decay_scaled(x, lw, r, kk, k, b_):
        e_in = jnp.exp(x)
        e_inv = jnp.exp(-x)
        return (e_in, e_in[CH - 1:CH, :] * e_inv, r * e_in, kk * jnp.exp(x - lw),
                b_ * e_inv, k * e_inv)

    e_in, e_rem, rt, kt, b_inv, k_inv = zip(*_each(decay_scaled, gam, logw, r_s, kk_s, kmod, bb))

    a_all = _each(
        lambda kt_, rt_, b_, k_: _mm_nt(jnp.concatenate([kt_, rt_], axis=0),
                                        jnp.concatenate([bd(b_), bd(k_)], axis=0)),
        kt, rt, b_inv, k_inv)
    a_kb = [jnp.where(strict, a[:CH, :LANE], 0.0) for a in a_all]
    a_kk = [jnp.where(strict, a[:CH, LANE:], 0.0) for a in a_all]
    a_rb = [jnp.where(incl, a[CH:, :LANE], 0.0) for a in a_all]
    a_rk = [jnp.where(incl, a[CH:, LANE:], 0.0) for a in a_all]
    t_inv = yield a_kb

    av = _each(lambda kk_, rk_, v: _mm(jnp.concatenate([kk_, rk_], axis=0), bd(v)),
               a_kk, a_rk, v_s)
    yield
    tx = _each(lambda t, kt_, av_: _mm(t, jnp.concatenate([bd(kt_), bd(av_[:CH])], axis=1)),
               t_inv, kt, av)
    w_nat = [x[:, :LANE] for x in tx]
    u0 = [x[:, LANE:] for x in tx]
    y0 = [x[CH:] for x in av]

    yield
    s_old = [s_ref[ri, g] for ri, g in items]
    wh = _each(lambda w, r, s: _mm_nt(jnp.concatenate([w, r], axis=0), s), w_nat, rt, s_old)
    yield
    u = _each(lambda wh_, u0_: wh_[:CH] + u0_, wh, u0)
    arbu = _each(lambda a, u_: _mm(a, bd(u_)), a_rb, u)
    yield
    y = _each(lambda wh_, y0_, x: wh_[CH:] + y0_ - x, wh, y0, arbu)
    s_add = _each(
        lambda v, u_, k, b, e: _mm_tn(jnp.concatenate([v, u_], axis=0),
                                      jnp.concatenate([k * e, -(b * e)], axis=0)),
        v_s, u, kmod, bb, e_rem)
    for (ri, g), s, e, add in zip(items, s_old, e_in, s_add):
        s_ref[ri, g] = s * e[CH - 1:CH, :] + jnp.where(same_head, add, 0.0)

    yield
    mean = _each(lambda y_: seg_sum(y_) * (1.0 / RW_HEAD), y)
    dev = _each(lambda y_, m_: y_ - m_, y, mean)
    var = _each(lambda d: seg_sum(d * d) * (1.0 / RW_HEAD), dev)
    bonus = _each(lambda r, k, item: seg_sum(r * k * rk_ref[item[1]]), r_s, kmod, items)

    def finish(d, var_, bonus_, v, item):
        ri, g = item
        yn = d * lax.rsqrt(var_ + RW_GN_EPS) * gnw_ref[g] + gnb_ref[g] + bonus_ * v
        gate = p_ref[3 * RW_G + 1 + g, ri]
        y_ref[g, ri] = (yn * _silu(gate)).astype(y_ref.dtype)

    _each(finish, dev, var, bonus, v_s, items)


def _gdn_rows(p_ref, cw_ref, alog_ref, dtb_ref, nw_ref, y_ref, s_ref, c_ref, *, row_ids):
    t_i, s_i, lo = _pair_time_index()
    strict = t_i > s_i
    incl = t_i >= s_i
    lo_row = lo[:1]
    tri = _tri_ones()
    zero = jnp.zeros((CH, LANE), F32)
    heads = [(ri, h) for ri in row_ids for h in range(DN_HEADS)]
    pairs = [(n, ri, hp) for n, ri in enumerate(row_ids) for hp in range(DN_HEADS // 2)]
    first = lambda xs: xs[0::2]
    second = lambda xs: xs[1::2]

    def conv_silu(group, ri):
        x = p_ref[group, ri]
        c_ref[ri, group, SUBLANE:, :] = x
        acc = x * cw_ref[CONV_W - 1, group]
        for back in range(1, CONV_W):
            acc = acc + (c_ref[ri, group, SUBLANE - back:SUBLANE - back + CH, :]
                         * cw_ref[CONV_W - 1 - back, group])
        return _silu(acc)

    def l2n(x, scale=1.0):
        return x * (lax.rsqrt(jnp.sum(x * x, axis=-1, keepdims=True) + L2_EPS) * scale)

    beta_all, gc_all, gc_t = {}, {}, {}
    for ri in row_ids:
        ba = p_ref[4 * DN_HEADS, ri]
        beta_all[ri] = _sigmoid(ba)
        z = ba + dtb_ref[...]
        softplus = jnp.maximum(z, 0.0) + jnp.log1p(jnp.exp(-jnp.abs(z)))
        gc_all[ri] = _cumsum_rows(tri, -jnp.exp(alog_ref[...]) * softplus)
        gc_t[ri] = jnp.concatenate([gc_all[ri], gc_all[ri]], axis=0).T

    q = [l2n(conv_silu(h, ri), DN_HEAD ** -0.5) for ri, h in heads]
    k = [l2n(conv_silu(DN_HEADS + h, ri)) for ri, h in heads]
    v = [conv_silu(2 * DN_HEADS + h, ri) for ri, h in heads]
    beta = [beta_all[ri][:, h:h + 1] for ri, h in heads]
    gc = [gc_all[ri][:, DN_HEADS + h:DN_HEADS + h + 1] for ri, h in heads]
    kb = _each(lambda x, b: x * b, k, beta)
    vb = _each(lambda x, b: x * b, v, beta)
    e_gc = _each(jnp.exp, gc)
    qe = _each(lambda x, e: x * e, q, e_gc)
    g_last = [x[CH - 1:CH, :] for x in gc]

    def decay_mask(item):
        n, ri, hp = item
        col = jnp.where(lo, gc[n * DN_HEADS + 2 * hp], gc[n * DN_HEADS + 2 * hp + 1])
        r1 = DN_HEADS + 2 * hp
        rowv = jnp.where(lo_row, gc_t[ri][r1:r1 + 1, :], gc_t[ri][r1 + 1:r1 + 2, :])
        return jnp.where(incl, jnp.exp(col - rowv), 0.0)

    decay = _each(decay_mask, pairs)

    def diag2(a, b):
        return jnp.concatenate([jnp.concatenate([a, zero], axis=1),
                                jnp.concatenate([zero, b], axis=1)], axis=0)

    scores = _each(
        lambda kb1, kb2, q1, q2, k1, k2: _mm_nt(
            jnp.concatenate([jnp.concatenate([kb1, kb2], axis=1),
                             jnp.concatenate([q1, q2], axis=1)], axis=0), diag2(k1, k2)),
        first(kb), second(kb), first(q), second(q), first(k), second(k))
    m_mat = _each(lambda s, d: jnp.where(strict, s[:CH] * d, 0.0), scores, decay)
    attn = _each(lambda s, d: jnp.where(incl, s[CH:] * d, 0.0), scores, decay)
    t_inv = yield m_mat

    kbe = _each(lambda x, e: x * e, kb, e_gc)
    uw = _each(
        lambda t, vb1, vb2, kbe1, kbe2: _mm(t, jnp.concatenate([
            jnp.concatenate([vb1, kbe1, zero, zero], axis=1),
            jnp.concatenate([zero, zero, vb2, kbe2], axis=1)], axis=0)),
        t_inv, first(vb), second(vb), first(kbe), second(kbe))

    yield
    s_old = [s_ref[ri, h] for ri, h in heads]
    u_h = [uw[i // 2][:, (i % 2) * 2 * LANE:(i % 2) * 2 * LANE + LANE] for i in range(len(heads))]
    w_h = [uw[i // 2][:, (i % 2) * 2 * LANE + LANE:(i % 2 + 1) * 2 * LANE]
           for i in range(len(heads))]
    ws = _each(lambda w, qe_, s: _mm(jnp.concatenate([w, qe_], axis=0), s), w_h, qe, s_old)
    yield
    v_new = _each(lambda u_, ws_: u_ - ws_[:CH], u_h, ws)
    o2 = _each(lambda a, vn1, vn2: _mm(a, diag2(vn1, vn2)),
               attn, first(v_new), second(v_new))

    yield
    k_dec = _each(lambda x, gl, g: x * jnp.exp(gl - g), k, g_last, gc)
    s_add = _each(_mm_tn, k_dec, v_new)
    yield
    for i, (ri, h) in enumerate(heads):
        s_ref[ri, h] = s_old[i] * jnp.exp(g_last[i]) + s_add[i]
        o = ws[i][CH:] + o2[i // 2][:, (i % 2) * LANE:(i % 2 + 1) * LANE]
        o = o * lax.rsqrt(jnp.mean(o * o, axis=-1, keepdims=True) + NORM_EPS) * nw_ref[...]
        zg = p_ref[3 * DN_HEADS + h, ri]
        y_ref[h, ri] = (o * _silu(zg)).astype(y_ref.dtype)


def _mix_kernel(p_ref, mu_ref, w0_ref, a0_ref, kk_ref, ka_ref, rk_ref, gnw_ref, gnb_ref, w2a2_ref,
                cw_ref, alog_ref, dtb_ref, nw_ref, rw_cin_ref, rw_s0_ref, dn_cin_ref, dn_s0_ref,
                ya_ref, yb_ref, rw_cout_ref, rw_sout_ref, dn_cout_ref, dn_sout_ref,
                rw_s_ref, rw_c_ref, dn_s_ref, dn_c_ref, *, n_chunks, rows):
    c = pl.program_id(1)
    dn_p_ref = p_ref.at[RW_PGROUPS:P_GROUPS]

    @pl.when(c == 0)
    def _():
        for ri in range(rows):
            rw_s_ref[ri] = rw_s0_ref[...]
            rw_c_ref[ri, :, 0:SUBLANE, :] = rw_cin_ref[...]
            dn_s_ref[ri] = dn_s0_ref[...]
            dn_c_ref[ri, :, 0:SUBLANE, :] = dn_cin_ref[...]

    row_ids = tuple(range(rows))
    mixers = [
        _gdn_rows(dn_p_ref, cw_ref, alog_ref, dtb_ref, nw_ref, yb_ref, dn_s_ref, dn_c_ref,
                  row_ids=row_ids),
        _rwkv_rows(p_ref, mu_ref, w0_ref, a0_ref, kk_ref, ka_ref, rk_ref, gnw_ref, gnb_ref,
                   w2a2_ref, ya_ref, rw_s_ref, rw_c_ref, row_ids=row_ids),
    ]
    wanted = [next(mixer) for mixer in mixers]
    inverses = _tri_inverse_each([mat for mats in wanted for mat in mats])
    live = []
    for mixer, mats in zip(mixers, wanted):
        mixer.send(inverses[:len(mats)])
        inverses = inverses[len(mats):]
        live.append(mixer)
    while live:
        for mixer in list(live):
            try:
                next(mixer)
            except StopIteration:
                live.remove(mixer)

    for ri in range(rows):
        rw_c_ref[ri, :, 0:SUBLANE, :] = p_ref[0:3 * RW_G + 1, ri, CH - SUBLANE:CH, :]
        dn_c_ref[ri, :, 0:SUBLANE, :] = dn_p_ref[0:3 * DN_HEADS, ri, CH - SUBLANE:CH, :]

    @pl.when(c == n_chunks - 1)
    def _():
        rw_cout_ref[...] = rw_c_ref[rows - 1, :, 0:SUBLANE, :]
        rw_sout_ref[...] = rw_s_ref[rows - 1]
        dn_cout_ref[...] = dn_c_ref[rows - 1, :, 0:SUBLANE, :]
        dn_sout_ref[...] = dn_s_ref[rows - 1]


def _mix(p, rw_prm, dn_prm, rw_carry, rw_state, dn_carry, dn_state, rows):
    _, batch, seq, _ = p.shape
    n_chunks = seq // CH
    full = lambda shape: pl.BlockSpec(shape, lambda b, c: (0,) * len(shape))
    rw_carry_n = 3 * RW_G + 1
    dn_carry_n = 3 * DN_HEADS
    seq_block = lambda groups: pl.BlockSpec((groups, rows, CH, LANE), lambda b, c: (0, b, c, 0))
    rw_group = full((RW_G, 1, LANE))
    return pl.pallas_call(
        functools.partial(_mix_kernel, n_chunks=n_chunks, rows=rows),
        grid=(batch // rows, n_chunks),
        in_specs=[
            seq_block(P_GROUPS),
            full((rw_carry_n, 1, LANE)),
            rw_group, rw_group, rw_group, rw_group, rw_group, rw_group, rw_group,
            full((2 * RW_LORA, 2 * RW_WIDTH)),
            full((CONV_W, dn_carry_n, 1, LANE)),
            full((1, LANE)), full((1, LANE)), full((1, LANE)),
            full((rw_carry_n, SUBLANE, LANE)), full((RW_G, LANE, LANE)),
            full((dn_carry_n, SUBLANE, LANE)), full((DN_HEADS, DN_HEAD, DN_HEAD)),
        ],
        out_specs=[
            seq_block(RW_G), seq_block(DN_HEADS),
            full((rw_carry_n, SUBLANE, LANE)), full((RW_G, LANE, LANE)),
            full((dn_carry_n, SUBLANE, LANE)), full((DN_HEADS, DN_HEAD, DN_HEAD)),
        ],
        out_shape=[
            jax.ShapeDtypeStruct((RW_G, batch, seq, LANE), BF16),
            jax.ShapeDtypeStruct((DN_HEADS, batch, seq, LANE), BF16),
            jax.ShapeDtypeStruct((rw_carry_n, SUBLANE, LANE), F32),
            jax.ShapeDtypeStruct((RW_G, LANE, LANE), F32),
            jax.ShapeDtypeStruct((dn_carry_n, SUBLANE, LANE), F32),
            jax.ShapeDtypeStruct((DN_HEADS, DN_HEAD, DN_HEAD), F32),
        ],
        scratch_shapes=[
            pltpu.VMEM((rows, RW_G, LANE, LANE), F32),
            pltpu.VMEM((rows, rw_carry_n, SUBLANE + CH, LANE), F32),
            pltpu.VMEM((rows, DN_HEADS, DN_HEAD, DN_HEAD), F32),
            pltpu.VMEM((rows, dn_carry_n, SUBLANE + CH, LANE), F32),
        ],
        compiler_params=pltpu.CompilerParams(
            dimension_semantics=("arbitrary", "arbitrary"), vmem_limit_bytes=VMEM_LIMIT),
        name="mixers",
    )(p, rw_prm["mu"], rw_prm["w0"], rw_prm["a0"], rw_prm["k_k"], rw_prm["k_a"], rw_prm["r_k"],
      rw_prm["gn_w"], rw_prm["gn_b"], rw_prm["w2a2"], dn_prm["conv_w"], dn_prm["a_log"],
      dn_prm["dt_bias"], dn_prm["norm_w"], rw_carry, rw_state, dn_carry, dn_state)


def _outproj_kernel(ya_ref, yb_ref, x_ref, w_ref, fnw_ref, o_ref):
    y = jnp.concatenate([ya_ref[g] for g in range(RW_G)] + [yb_ref[h] for h in range(DN_HEADS)],
                        axis=1)
    hid = x_ref[...] + jnp.dot(y, w_ref[...], preferred_element_type=F32)
    ms = jnp.mean(hid * hid, axis=-1, keepdims=True)
    o_ref[...] = hid * lax.rsqrt(ms + NORM_EPS) * fnw_ref[...]


def _outproj(ya, yb, x2d, w_out, fnw, tm):
    m, d = x2d.shape
    return pl.pallas_call(
        _outproj_kernel,
        grid=(m // tm,),
        in_specs=[
            pl.BlockSpec((RW_G, tm, LANE), lambda i: (0, i, 0)),
            pl.BlockSpec((DN_HEADS, tm, LANE), lambda i: (0, i, 0)),
            pl.BlockSpec((tm, d), lambda i: (i, 0)),
            pl.BlockSpec((RW_WIDTH + DN_WIDTH, d), lambda i: (0, 0)),
            pl.BlockSpec((1, d), lambda i: (0, 0)),
        ],
        out_specs=pl.BlockSpec((tm, d), lambda i: (i, 0)),
        out_shape=jax.ShapeDtypeStruct((m, d), F32),
        compiler_params=pltpu.CompilerParams(
            dimension_semantics=("arbitrary",), vmem_limit_bytes=VMEM_LIMIT),
        name="outproj",
    )(ya, yb, x2d, w_out, fnw)


def _row_tile(m, cap):
    t = cap
    while m % t:
        t //= 2
    return t


def kernel(x, meta_tokens, norm_w, w_in, rw_shift_mu, rw_w0, rw_w2, rw_a0, rw_a2, rw_k_k, rw_k_a,
           rw_r_k, rw_gn_w, rw_gn_b, dn_conv_w, dn_A_log, dn_dt_bias, dn_norm_w, w_out,
           final_norm_w):
    batch, seq, d = x.shape
    assert seq % CH == 0 and norm_w.shape[0] == 1
    m = batch * seq

    tn = IN_COL_TILE
    wi = w_in[0].T.astype(BF16)
    c_b = 3 * RW_WIDTH + 2 * RW_LORA + RW_WIDTH + 3 * DN_WIDTH
    c_z = c_b + 2 * DN_HEADS
    c_tail = (P_GROUPS * LANE // tn - 2) * tn
    assert c_tail <= c_b and c_tail % LANE == 0
    w_tail = jnp.concatenate([
        wi[c_tail:c_b], wi[c_z:], wi[c_b:c_z],
        jnp.zeros((LANE - 2 * DN_HEADS, d), wi.dtype)], axis=0)
    grp = lambda t: t.reshape(-1, 1, LANE)
    zeros_l = jnp.zeros((RW_LORA, RW_WIDTH), F32)
    rw_prm = {
        "mu": grp(rw_shift_mu[0]),
        "w0": grp(rw_w0[0]), "a0": grp(rw_a0[0]), "k_k": grp(rw_k_k[0]), "k_a": grp(rw_k_a[0]),
        "r_k": grp(rw_r_k[0]), "gn_w": grp(rw_gn_w[0]), "gn_b": grp(rw_gn_b[0]),
        "w2a2": jnp.concatenate([jnp.concatenate([rw_w2[0], zeros_l], axis=1),
                                 jnp.concatenate([zeros_l, rw_a2[0]], axis=1)], axis=0).astype(BF16),
    }
    lane_vec = lambda t: jnp.zeros((1, LANE), F32).at[0, DN_HEADS:2 * DN_HEADS].set(t)
    dn_prm = {
        "conv_w": dn_conv_w[0].reshape(CONV_W, 3 * DN_HEADS, 1, LANE),
        "a_log": lane_vec(dn_A_log[0]), "dt_bias": lane_vec(dn_dt_bias[0]),
        "norm_w": dn_norm_w[0].reshape(1, LANE),
    }
    nw = norm_w[0].reshape(1, d)

    meta_rows = jnp.concatenate([jnp.zeros((CH - N_META, d), x.dtype), meta_tokens.astype(x.dtype)],
                                axis=0)
    x2d = x.reshape(m, d)
    p, p_meta = _inproj(x2d, meta_rows, nw, wi, w_tail, _row_tile(m, IN_ROW_TILE), tn)
    zc = lambda n: jnp.zeros((n, SUBLANE, LANE), F32)
    _, _, rw_carry, rw_state, dn_carry, dn_state = _mix(
        p_meta.reshape(P_GROUPS, 1, CH, LANE), rw_prm, dn_prm, zc(3 * RW_G + 1),
        jnp.zeros((RW_G, LANE, LANE), F32), zc(3 * DN_HEADS),
        jnp.zeros((DN_HEADS, DN_HEAD, DN_HEAD), F32), 1)

    p = p.reshape(P_GROUPS, batch, seq, LANE)
    ya, yb, _, _, _, _ = _mix(p, rw_prm, dn_prm, rw_carry, rw_state, dn_carry, dn_state,
                              MIX_ROWS if batch % MIX_ROWS == 0 else 1)
    out = _outproj(ya.reshape(RW_G, m, LANE), yb.reshape(DN_HEADS, m, LANE), x2d,
                   w_out[0].astype(BF16), final_norm_w.reshape(1, d), _row_tile(m, OUT_ROW_TILE))
    return out.reshape(batch, seq, d)
```
